```python
import math
import jax, jax.numpy as jnp
from jax import lax
import numpy as np

D_MODEL = 1024
BATCH = 4
SEQ = 4096
DEPTH = 1
DEC_BATCH = 128
DEC_SEQ = 1
PAST_LEN = 8192
PAGE_SIZE = 128

GDN_HEADS = 4
GDN_DK = 128
GDN_DV = 128
GDN_CONV = 4
GDN_CHUNK = 64
GDN_QK = GDN_HEADS * GDN_DK
GDN_VW = GDN_HEADS * GDN_DV
GDN_CONV_CH = 2 * GDN_QK + GDN_VW
GDN_COLS = GDN_CONV_CH + GDN_VW + 2 * GDN_HEADS

ATT_GROUPS = ((128, 1), (512, 4), (2048, 16))
N_ATT_GROUPS = len(ATT_GROUPS)
ATT_HEADS = 8
ATT_HD = 64
ATT_GROUP_COLS = 3 * ATT_HEADS * ATT_HD
ROT_DIM = ATT_HD // 4
ROPE_THETA = 500000.0

IN_COLS = GDN_COLS + N_ATT_GROUPS * ATT_GROUP_COLS
MIX_WIDTH = GDN_VW + ATT_HEADS * ATT_HD

MOE_GROUPS = 4
MOE_PER_GROUP = 8
MOE_EXPERTS = MOE_GROUPS * MOE_PER_GROUP
MOE_TOPK = 2
MOE_FF = 256
EPS = 1e-6

kernel_name = 'hybrid_gdn_dilated_swa_hmoe_step'


def rmsnorm(x, w):
    xf = x.astype(jnp.float32)
    y = xf * lax.rsqrt(jnp.mean(xf * xf, axis=-1, keepdims=True) + EPS)
    return (y * w.astype(jnp.float32)).astype(x.dtype)


def l2norm(x):
    xf = x.astype(jnp.float32)
    return xf * lax.rsqrt(jnp.sum(xf * xf, axis=-1, keepdims=True) + EPS)


def rope(x, pos):
    half = ROT_DIM // 2
    inv = jnp.exp(-math.log(ROPE_THETA) * jnp.arange(half, dtype=jnp.float32) * (2.0 / ROT_DIM))
    ang = pos.astype(jnp.float32)[:, None] * inv[None, :]
    cos = jnp.cos(ang)[:, None, :]
    sin = jnp.sin(ang)[:, None, :]
    xf = x.astype(jnp.float32)
    x1 = xf[..., :half]
    x2 = xf[..., half:ROT_DIM]
    out = jnp.concatenate([x1 * cos - x2 * sin, x2 * cos + x1 * sin, xf[..., ROT_DIM:]], axis=-1)
    return out.astype(x.dtype)


def causal_conv(u, buf, w):
    full = jnp.concatenate([buf.astype(u.dtype), u], axis=1)
    rhs = jnp.transpose(w).astype(u.dtype)[:, None, :]
    out = lax.conv_general_dilated(full, rhs, window_strides=(1,), padding='VALID',
                                   dimension_numbers=('NWC', 'WIO', 'NWC'),
                                   feature_group_count=u.shape[-1])
    return jax.nn.silu(out), full[:, -(GDN_CONV - 1):]


def gdn_chunked(q, k, v, g, beta, s0):
    f32 = jnp.float32
    b, l, h, _ = k.shape
    dv = v.shape[-1]
    c = min(GDN_CHUNK, l)
    n = -(-l // c)
    pad = n * c - l

    def prep(t):
        t = t.astype(f32)
        t = jnp.pad(t, [(0, 0), (0, pad)] + [(0, 0)] * (t.ndim - 2))
        t = jnp.moveaxis(t.reshape((b, n, c) + t.shape[2:]), 1, 0)
        return jnp.swapaxes(t, 2, 3)

    qc, kc, vc, gc, bc = prep(q), prep(k), prep(v), prep(g), prep(beta)
    gcum = jnp.cumsum(gc, axis=-1)
    causal = jnp.tril(jnp.ones((c, c), dtype=bool))
    strict = jnp.tril(jnp.ones((c, c), dtype=bool), -1)
    decay = jnp.exp(jnp.where(causal, gcum[..., :, None] - gcum[..., None, :], -jnp.inf))
    kb = kc * bc[..., None]
    a_low = jnp.where(strict, jnp.einsum('nbhid,nbhjd->nbhij', kb, kc) * decay, 0.0)
    tmat = a_low + jnp.eye(c, dtype=f32)
    u = lax.linalg.triangular_solve(tmat, vc * bc[..., None], left_side=True, lower=True, unit_diagonal=True)
    w = lax.linalg.triangular_solve(tmat, kb * jnp.exp(gcum)[..., None], left_side=True, lower=True, unit_diagonal=True)
    intra = jnp.where(causal, jnp.einsum('nbhid,nbhjd->nbhij', qc, kc) * decay, 0.0)

    def step(s, inp):
        q_i, k_i, u_i, w_i, a_i, g_i = inp
        v_new = u_i - jnp.einsum('bhck,bhkv->bhcv', w_i, s)
        o_i = (jnp.einsum('bhck,bhkv->bhcv', q_i * jnp.exp(g_i)[..., None], s)
               + jnp.einsum('bhij,bhjv->bhiv', a_i, v_new))
        g_last = g_i[..., -1]
        s = (s * jnp.exp(g_last)[..., None, None]
             + jnp.einsum('bhck,bhcv->bhkv', k_i * jnp.exp(g_last[..., None] - g_i)[..., None], v_new))
        return s, o_i

    s_fin, o = lax.scan(step, s0.astype(f32), (qc, kc, u, w, intra, gcum))
    o = o.transpose(1, 0, 3, 2, 4).reshape(b, n * c, h, dv)[:, :l]
    return o, s_fin


def dilated_attn_prompt(q, k, v, window, dilation):
    b, s, h, d = q.shape
    span = window // dilation
    m = s // dilation
    nb = -(-m // span)
    mp = nb * span

    def sub(t):
        t = t.reshape(b, m, dilation, h, d).transpose(0, 2, 1, 3, 4)
        t = jnp.pad(t, ((0, 0), (0, 0), (0, mp - m), (0, 0), (0, 0)))
        return t.reshape(b, dilation, nb, span, h, d)

    def band(t):
        prev = jnp.pad(t, ((0, 0), (0, 0), (1, 0), (0, 0), (0, 0), (0, 0)))[:, :, :nb]
        return jnp.concatenate([prev, t], axis=3)

    qs, ks, vs = sub(q), sub(k), sub(v)
    kb, vb = band(ks), band(vs)
    scores = jnp.einsum('brnqhd,brnkhd->brnhqk', qs, kb, preferred_element_type=jnp.float32) * (d ** -0.5)
    qi = jnp.arange(span)[:, None]
    kj = jnp.arange(2 * span)[None, :]
    dist = span + qi - kj
    in_band = (dist >= 0) & (dist <= span)
    has_prev = (jnp.arange(nb) > 0)[:, None, None] | (kj >= span)[None]
    mask = in_band[None] & has_prev
    scores = jnp.where(mask[None, None, :, None], scores, -jnp.inf)
    mx = jnp.max(scores, axis=-1, keepdims=True)
    p = jnp.exp(scores - mx)
    den = jnp.sum(p, axis=-1, keepdims=True)
    o = jnp.einsum('brnhqk,brnkhd->brnqhd', p / den, vb.astype(jnp.float32))
    lse = (mx + jnp.log(den))[..., 0]
    o = o.reshape(b, dilation, mp, h, d)[:, :, :m].transpose(0, 2, 1, 3, 4).reshape(b, s, h, d)
    lse = lse.transpose(0, 1, 2, 4, 3).reshape(b, dilation, mp, h)[:, :, :m].transpose(0, 2, 1, 3).reshape(b, s, h)
    return o, lse


def dilated_attn_sample(q, k, v, kv_buf, window, dilation):
    b, l, h, d = q.shape
    wb = kv_buf.shape[1]
    span = window // dilation
    kv_all = jnp.concatenate([kv_buf, jnp.stack([k, v], axis=2).astype(kv_buf.dtype)], axis=1)
    idx = wb + jnp.arange(l)[:, None] - dilation * jnp.arange(span + 1)[None, :]
    valid = idx >= 0
    kv_g = kv_all[:, jnp.maximum(idx, 0)]
    scores = jnp.einsum('blhd,bljhd->blhj', q, kv_g[:, :, :, 0], preferred_element_type=jnp.float32) * (d ** -0.5)
    scores = jnp.where(valid[None, :, None, :], scores, -jnp.inf)
    mx = jnp.max(scores, axis=-1, keepdims=True)
    p = jnp.exp(scores - mx)
    den = jnp.sum(p, axis=-1, keepdims=True)
    o = jnp.einsum('blhj,bljhd->blhd', p / den, kv_g[:, :, :, 1].astype(jnp.float32))
    lse = (mx + jnp.log(den))[..., 0]
    new_buf = kv_all[:, -min(window, wb + l):]
    return o, lse, new_buf


def hier_moe(x, w_router_group, w_router_expert, w_gate_up, w_down):
    b, l, dm = x.shape
    t = x.reshape(b * l, dm)
    nt = b * l
    g_logits = jnp.einsum('td,dg->tg', t, w_router_group, preferred_element_type=jnp.float32)
    g_prob = jax.nn.softmax(g_logits, axis=-1)
    _, g_idx = lax.top_k(g_logits, 1)
    p_group = jnp.take_along_axis(g_prob, g_idx, axis=-1)
    e_logits = jnp.einsum('td,de->te', t, w_router_expert, preferred_element_type=jnp.float32)
    e_logits = e_logits.reshape(nt, MOE_GROUPS, MOE_PER_GROUP)
    e_in = jnp.take_along_axis(e_logits, g_idx[:, :, None], axis=1)[:, 0]
    e_top, e_idx = lax.top_k(e_in, MOE_TOPK)
    weights = jax.nn.softmax(e_top, axis=-1) * p_group
    expert = g_idx * MOE_PER_GROUP + e_idx
    gate = jnp.einsum('tk,tke->te', weights, jax.nn.one_hot(expert, MOE_EXPERTS, dtype=jnp.float32))
    y = jnp.zeros((nt, dm), jnp.float32)
    for e in range(MOE_EXPERTS):
        gu = jnp.dot(t, w_gate_up[e])
        hid = jax.nn.silu(gu[:, :MOE_FF]) * gu[:, MOE_FF:]
        y = y + gate[:, e:e + 1] * jnp.dot(hid, w_down[e], preferred_element_type=jnp.float32)
    return y.astype(x.dtype).reshape(b, l, dm)


def hybrid_layer(x, pos, gdn_s0, conv_buf, kv_bufs, norm1_w, w_in, conv_w, a_log, dt_bias, gdn_norm_w,
                 q_norm_w, k_norm_w, w_out, norm2_w, w_router_group, w_router_expert, w_gate_up, w_down):
    b, l, _ = x.shape
    xn = rmsnorm(x, norm1_w)
    proj = jnp.einsum('bld,dc->blc', xn, w_in)
    c0 = GDN_CONV_CH
    qkv_a = proj[..., :c0]
    z = proj[..., c0:c0 + GDN_VW].reshape(b, l, GDN_HEADS, GDN_DV)
    b_raw = proj[..., c0 + GDN_VW:c0 + GDN_VW + GDN_HEADS]
    a_raw = proj[..., c0 + GDN_VW + GDN_HEADS:GDN_COLS]
    att = proj[..., GDN_COLS:].reshape(b, l, N_ATT_GROUPS, 3, ATT_HEADS, ATT_HD)

    qkv_c, conv_new = causal_conv(qkv_a, conv_buf, conv_w)
    qa = l2norm(qkv_c[..., :GDN_QK].reshape(b, l, GDN_HEADS, GDN_DK)) * (GDN_DK ** -0.5)
    ka = l2norm(qkv_c[..., GDN_QK:2 * GDN_QK].reshape(b, l, GDN_HEADS, GDN_DK))
    va = qkv_c[..., 2 * GDN_QK:].reshape(b, l, GDN_HEADS, GDN_DV)
    beta = jax.nn.sigmoid(b_raw.astype(jnp.float32))
    g = -jnp.exp(a_log.astype(jnp.float32)) * jax.nn.softplus(a_raw.astype(jnp.float32) + dt_bias.astype(jnp.float32))
    oa, s_new = gdn_chunked(qa, ka, va, g, beta, gdn_s0)
    oa = (rmsnorm(oa, gdn_norm_w) * jax.nn.silu(z.astype(jnp.float32))).astype(x.dtype)

    outs, lses, new_bufs = [], [], []
    for gi in range(N_ATT_GROUPS):
        window, dilation = ATT_GROUPS[gi]
        qg = rope(rmsnorm(att[:, :, gi, 0], q_norm_w[gi]), pos)
        kg = rope(rmsnorm(att[:, :, gi, 1], k_norm_w[gi]), pos)
        vg = att[:, :, gi, 2]
        if kv_bufs is None:
            o_g, lse_g = dilated_attn_prompt(qg, kg, vg, window, dilation)
            buf_g = jnp.stack([kg, vg], axis=2)[:, -min(window, l):]
        else:
            o_g, lse_g, buf_g = dilated_attn_sample(qg, kg, vg, kv_bufs[gi], window, dilation)
        outs.append(o_g)
        lses.append(lse_g)
        new_bufs.append(buf_g)
    wgt = jax.nn.softmax(jnp.stack(lses, axis=0), axis=0)
    ob = jnp.sum(wgt[..., None] * jnp.stack(outs, axis=0), axis=0)
    ob = ob.reshape(b, l, ATT_HEADS * ATT_HD).astype(x.dtype)

    mix = jnp.concatenate([oa.reshape(b, l, GDN_VW), ob], axis=-1)
    h = x + jnp.einsum('blm,md->bld', mix, w_out)
    y = h + hier_moe(rmsnorm(h, norm2_w), w_router_group, w_router_expert, w_gate_up, w_down)
    return y, s_new.astype(gdn_s0.dtype), conv_new, new_bufs


def setup_inputs(seed: int = 0) -> dict:
    key = jax.random.key(seed)
    ks = jax.random.split(key, 24)
    f32 = jnp.float32

    def nrm(k, shape, scale):
        return jax.random.normal(k, shape, f32) * scale

    def kv(k, window):
        return nrm(k, (DEPTH, DEC_BATCH, min(window, PAST_LEN), 2, ATT_HEADS, ATT_HD), 1.0)

    dt = jnp.exp(jax.random.uniform(ks[11], (DEPTH, GDN_HEADS), f32, math.log(1e-3), math.log(1e-1)))
    return {
        'x_prompt': nrm(ks[0], (BATCH, SEQ, D_MODEL), 1.0),
        'x_sample': nrm(ks[1], (DEC_BATCH, DEC_SEQ, D_MODEL), 1.0),
        'state_gdn': nrm(ks[2], (DEPTH, DEC_BATCH, GDN_HEADS, GDN_DK, GDN_DV), 0.1),
        'state_conv': nrm(ks[3], (DEPTH, DEC_BATCH, GDN_CONV - 1, GDN_CONV_CH), 1.0),
        'cache_kv_w128': kv(ks[4], ATT_GROUPS[0][0]),
        'cache_kv_w512': kv(ks[5], ATT_GROUPS[1][0]),
        'cache_kv_w2048': kv(ks[6], ATT_GROUPS[2][0]),
        'norm1_w': 1.0 + nrm(ks[7], (DEPTH, D_MODEL), 0.02),
        'w_in': nrm(ks[8], (DEPTH, D_MODEL, IN_COLS), D_MODEL ** -0.5),
        'conv_w': nrm(ks[9], (DEPTH, GDN_CONV_CH, GDN_CONV), GDN_CONV ** -0.5),
        'a_log': jnp.log(jax.random.uniform(ks[10], (DEPTH, GDN_HEADS), f32, 1.0, 16.0)),
        'dt_bias': dt + jnp.log(-jnp.expm1(-dt)),
        'gdn_norm_w': 1.0 + nrm(ks[12], (DEPTH, GDN_DV), 0.02),
        'q_norm_w': 1.0 + nrm(ks[13], (DEPTH, N_ATT_GROUPS, ATT_HD), 0.02),
        'k_norm_w': 1.0 + nrm(ks[14], (DEPTH, N_ATT_GROUPS, ATT_HD), 0.02),
        'w_out': nrm(ks[15], (DEPTH, MIX_WIDTH, D_MODEL), MIX_WIDTH ** -0.5),
        'norm2_w': 1.0 + nrm(ks[16], (DEPTH, D_MODEL), 0.02),
        'w_router_group': nrm(ks[17], (DEPTH, D_MODEL, MOE_GROUPS), D_MODEL ** -0.5),
        'w_router_expert': nrm(ks[18], (DEPTH, D_MODEL, MOE_EXPERTS), D_MODEL ** -0.5),
        'w_gate_up': nrm(ks[19], (DEPTH, MOE_EXPERTS, D_MODEL, 2 * MOE_FF), D_MODEL ** -0.5),
        'w_down': nrm(ks[20], (DEPTH, MOE_EXPERTS, MOE_FF, D_MODEL), MOE_FF ** -0.5),
    }


def reference(x_prompt, x_sample, state_gdn, state_conv, cache_kv_w128, cache_kv_w512, cache_kv_w2048,
              norm1_w, w_in, conv_w, a_log, dt_bias, gdn_norm_w, q_norm_w, k_norm_w, w_out, norm2_w,
              w_router_group, w_router_expert, w_gate_up, w_down):
    bp, lp = x_prompt.shape[0], x_prompt.shape[1]
    ls = x_sample.shape[1]
    pos_p = jnp.arange(lp, dtype=jnp.int32)
    pos_s = PAST_LEN + jnp.arange(ls, dtype=jnp.int32)
    gdn0 = jnp.zeros((bp, GDN_HEADS, GDN_DK, GDN_DV), state_gdn.dtype)
    conv0 = jnp.zeros((bp, GDN_CONV - 1, GDN_CONV_CH), x_prompt.dtype)
    hp, hs = x_prompt, x_sample
    gdn_p, conv_p, w128_p, w512_p, w2048_p = [], [], [], [], []
    gdn_s, conv_s, w128_s, w512_s, w2048_s = [], [], [], [], []
    for li in range(DEPTH):
        wl = (norm1_w[li], w_in[li], conv_w[li], a_log[li], dt_bias[li], gdn_norm_w[li], q_norm_w[li],
              k_norm_w[li], w_out[li], norm2_w[li], w_router_group[li], w_router_expert[li],
              w_gate_up[li], w_down[li])
        hp, sg, sc, bufs = hybrid_layer(hp, pos_p, gdn0, conv0, None, *wl)
        gdn_p.append(sg)
        conv_p.append(sc)
        w128_p.append(bufs[0])
        w512_p.append(bufs[1])
        w2048_p.append(bufs[2])
        hs, sg, sc, bufs = hybrid_layer(hs, pos_s, state_gdn[li], state_conv[li],
                                        (cache_kv_w128[li], cache_kv_w512[li], cache_kv_w2048[li]), *wl)
        gdn_s.append(sg)
        conv_s.append(sc)
        w128_s.append(bufs[0])
        w512_s.append(bufs[1])
        w2048_s.append(bufs[2])
    state_gdn_prompt = jnp.stack(gdn_p)
    state_conv_prompt = jnp.stack(conv_p)
    kv_w128_prompt = jnp.stack(w128_p)
    kv_w512_prompt = jnp.stack(w512_p)
    kv_w2048_prompt = jnp.stack(w2048_p)
    state_gdn_sample = jnp.stack(gdn_s)
    state_conv_sample = jnp.stack(conv_s)
    kv_w128_sample = jnp.stack(w128_s)
    kv_w512_sample = jnp.stack(w512_s)
    kv_w2048_sample = jnp.stack(w2048_s)
    return (hp, hs, state_gdn_prompt, state_conv_prompt, kv_w128_prompt, kv_w512_prompt, kv_w2048_prompt,
            state_gdn_sample, state_conv_sample, kv_w128_sample, kv_w512_sample, kv_w2048_sample)
```

```python
import functools
import math

import jax
import jax.numpy as jnp
from jax import lax
from jax.experimental import pallas as pl
from jax.experimental.pallas import tpu as pltpu

F32 = jnp.float32
BF16 = jnp.bfloat16

LANES = 128
D_MODEL = 1024
GDN_HEADS = 4
GDN_D = 128
GDN_CONV = 4
GDN_CH = 3 * GDN_HEADS * GDN_D
GDN_CB = GDN_CH // LANES
GDN_BLOCKS = 16
ATT_GROUPS = ((128, 1), (512, 4), (2048, 16))
ATT_HEADS = 8
ATT_HD = 64
ATT_W = ATT_HEADS * ATT_HD
ATT_CB = ATT_W // LANES
SPAN = 128
ROT_DIM = 16
ROT_HALF = ROT_DIM // 2
ROPE_THETA = 500000.0
MOE_GROUPS = 4
MOE_PER_GROUP = 8
MOE_EXPERTS = 32
MOE_FF = 256
EPS = 1e-6
PAST_LEN = 8192
NEG = -1e30
N_MAIN_BLOCKS = GDN_BLOCKS + 3 * 3 * ATT_CB
VMEM_LIMIT = 48 * 1024 * 1024


def _dot(a, b):
    return jnp.dot(a, b, preferred_element_type=F32)


def _dot_nt(a, b):
    return lax.dot_general(a, b, (((1,), (1,)), ((), ())), preferred_element_type=F32)


def _bdot(a, b):
    return _dot(a.astype(BF16), b.astype(BF16))


def _split3(x):
    hi = x.astype(BF16)
    r1 = x - hi.astype(F32)
    mid = r1.astype(BF16)
    lo = (r1 - mid.astype(F32)).astype(BF16)
    return hi, mid, lo


def _dot_sel_left(sel_bf16, x):
    hi, mid, lo = _split3(x)
    return _dot(sel_bf16, hi) + _dot(sel_bf16, mid) + _dot(sel_bf16, lo)


def _dot_sel_right(x, sel_bf16):
    hi, mid, lo = _split3(x)
    return _dot(hi, sel_bf16) + _dot(mid, sel_bf16) + _dot(lo, sel_bf16)


def _silu(x):
    return x * jax.nn.sigmoid(x)


def _softplus(x):
    return jnp.maximum(x, 0.0) + jnp.log1p(jnp.exp(-jnp.abs(x)))


def _params(sem, vmem=VMEM_LIMIT):
    return pltpu.CompilerParams(dimension_semantics=sem, vmem_limit_bytes=vmem)


def _inproj_kernel(x_ref, nw_ref, w_ref, wba_ref, out_ref, ba_ref, xn_ref, *, nblk):
    @pl.when(pl.program_id(1) == 0)
    def _():
        x = x_ref[...]
        ms = jnp.mean(x * x, axis=-1, keepdims=True)
        xn = ((x * lax.rsqrt(ms + EPS)) * nw_ref[...]).astype(BF16)
        xn_ref[...] = xn
        ba_ref[...] = _dot(xn, wba_ref[...])

    res = _dot(xn_ref[...], w_ref[...])
    for jj in range(nblk):
        out_ref[jj] = res[:, jj * LANES:(jj + 1) * LANES]


def _inproj(x2d, norm_w, w_main, w_ba, tm, tn=512):
    t = x2d.shape[0]
    ncol = w_main.shape[1]
    nblk = tn // LANES
    return pl.pallas_call(
        functools.partial(_inproj_kernel, nblk=nblk),
        grid=(t // tm, ncol // tn),
        in_specs=[
            pl.BlockSpec((tm, D_MODEL), lambda i, j: (i, 0)),
            pl.BlockSpec((1, D_MODEL), lambda i, j: (0, 0)),
            pl.BlockSpec((D_MODEL, tn), lambda i, j: (0, j)),
            pl.BlockSpec((D_MODEL, LANES), lambda i, j: (0, 0)),
        ],
        out_specs=[
            pl.BlockSpec((nblk, tm, LANES), lambda i, j: (j, i, 0)),
            pl.BlockSpec((tm, LANES), lambda i, j: (i, 0)),
        ],
        out_shape=[
            jax.ShapeDtypeStruct((ncol // LANES, t, LANES), F32),
            jax.ShapeDtypeStruct((t, LANES), F32),
        ],
        scratch_shapes=[pltpu.VMEM((tm, D_MODEL), BF16)],
        compiler_params=_params(("parallel", "arbitrary")),
        name="inproj",
    )(x2d, norm_w, w_main, w_ba)


def _gates(ba, alog_row, dtb_row):
    beta = jax.nn.sigmoid(ba)
    g = -jnp.exp(alog_row) * _softplus(ba + dtb_row)
    return beta, g


def _l2norm(x):
    return x * lax.rsqrt(jnp.sum(x * x, axis=-1, keepdims=True) + EPS)


def _gdn_out_norm(o, gnw, z):
    on = (o * lax.rsqrt(jnp.mean(o * o, axis=-1, keepdims=True) + EPS)) * gnw
    return on * _silu(z)


def _gdn_prompt_kernel(x_ref, ba_ref, cw_ref, alog_ref, dtb_ref, gnw_ref,
                       oa_ref, so_ref, co_ref, s_ref, cbuf_ref, *, nchunks, inv_passes):
    c = pl.program_id(1)
    ch = SPAN

    @pl.when(c == 0)
    def _():
        s_ref[...] = jnp.zeros_like(s_ref)
        cbuf_ref[:, 0:8, :] = jnp.zeros((GDN_CB, 8, LANES), F32)

    cw = cw_ref[...]
    qkv = []
    for cb in range(GDN_CB):
        cbuf_ref[cb, 8:8 + ch, :] = x_ref[cb]
        acc = None
        for tap in range(GDN_CONV):
            term = cw[tap:tap + 1, cb * LANES:(cb + 1) * LANES] * cbuf_ref[cb, 5 + tap:5 + tap + ch, :]
            acc = term if acc is None else acc + term
        qkv.append(_silu(acc))

    @pl.when(c == nchunks - 1)
    def _():
        for cb in range(GDN_CB):
            co_ref[0, :, cb * LANES:(cb + 1) * LANES] = cbuf_ref[cb, 8 + ch - 3:8 + ch, :]

    for cb in range(GDN_CB):
        cbuf_ref[cb, 0:8, :] = cbuf_ref[cb, ch:ch + 8, :]

    beta_t, g_t = _gates(ba_ref[...], alog_ref[...], dtb_ref[...])
    row = lax.broadcasted_iota(jnp.int32, (ch, ch), 0)
    col = lax.broadcasted_iota(jnp.int32, (ch, ch), 1)
    causal = row >= col
    strict = row > col
    tril = jnp.where(causal, 1.0, 0.0).astype(BF16)
    eye = jnp.where(row == col, 1.0, 0.0)
    gc = _dot_sel_left(tril, g_t)
    gct = jnp.transpose(gc)

    def mm(a, b):
        if inv_passes == 1:
            return _bdot(a, b)
        ah = a.astype(BF16)
        al = (a - ah.astype(F32)).astype(BF16)
        bh = b.astype(BF16)
        bl = (b - bh.astype(F32)).astype(BF16)
        return _dot(ah, bh) + _dot(ah, bl) + _dot(al, bh)

    for h in range(GDN_HEADS):
        gl = GDN_HEADS + h
        gcol = gc[:, gl:gl + 1]
        grow = gct[gl:gl + 1, :]
        bcol = beta_t[:, h:h + 1]
        decay = jnp.exp(jnp.where(causal, gcol - grow, NEG))
        q = _l2norm(qkv[h]) * (GDN_D ** -0.5)
        k = _l2norm(qkv[GDN_HEADS + h])
        v = qkv[2 * GDN_HEADS + h]
        kb = k * bcol
        kbf = k.astype(BF16)
        a_low = jnp.where(strict, _dot_nt(kb.astype(BF16), kbf) * decay, 0.0)
        m = -a_low
        p = eye + m
        for _ in range(6):
            m = mm(m, m)
            p = p + mm(p, m)
        eg = jnp.exp(gcol)
        u = mm(p, v * bcol)
        w = mm(p, kb * eg)
        intra = jnp.where(causal, _dot_nt(q.astype(BF16), kbf) * decay, 0.0)
        s = s_ref[h]
        sb = s.astype(BF16)
        v_new = u - _dot(w.astype(BF16), sb)
        vnb = v_new.astype(BF16)
        o = _dot((q * eg).astype(BF16), sb) + _dot(intra.astype(BF16), vnb)
        glast = gc[ch - 1:ch, gl:gl + 1]
        kdec = k * jnp.exp(glast - gcol)
        s_ref[h] = s * jnp.exp(glast) + _dot(jnp.transpose(kdec).astype(BF16), vnb)
        oa_ref[h] = _gdn_out_norm(o, gnw_ref[...], x_ref[3 * GDN_HEADS + h])

    @pl.when(c == nchunks - 1)
    def _():
        so_ref[0] = s_ref[...]


def _gdn_prompt(proj, ba, conv_wt, alog_row, dtb_row, gnw, batch, seq, inv_passes=3):
    nchunks = seq // SPAN
    t = batch * seq
    return pl.pallas_call(
        functools.partial(_gdn_prompt_kernel, nchunks=nchunks, inv_passes=inv_passes),
        grid=(batch, nchunks),
        in_specs=[
            pl.BlockSpec((GDN_BLOCKS, SPAN, LANES), lambda b, c: (0, b * nchunks + c, 0)),
            pl.BlockSpec((SPAN, LANES), lambda b, c: (b * nchunks + c, 0)),
            pl.BlockSpec((GDN_CONV, GDN_CH), lambda b, c: (0, 0)),
            pl.BlockSpec((1, LANES), lambda b, c: (0, 0)),
            pl.BlockSpec((1, LANES), lambda b, c: (0, 0)),
            pl.BlockSpec((1, LANES), lambda b, c: (0, 0)),
        ],
        out_specs=[
            pl.BlockSpec((GDN_HEADS, SPAN, LANES), lambda b, c: (0, b * nchunks + c, 0)),
            pl.BlockSpec((1, GDN_HEADS, GDN_D, GDN_D), lambda b, c: (b, 0, 0, 0)),
            pl.BlockSpec((1, GDN_CONV - 1, GDN_CH), lambda b, c: (b, 0, 0)),
        ],
        out_shape=[
            jax.ShapeDtypeStruct((GDN_HEADS, t, LANES), F32),
            jax.ShapeDtypeStruct((batch, GDN_HEADS, GDN_D, GDN_D), F32),
            jax.ShapeDtypeStruct((batch, GDN_CONV - 1, GDN_CH), F32),
        ],
        scratch_shapes=[
            pltpu.VMEM((GDN_HEADS, GDN_D, GDN_D), F32),
            pltpu.VMEM((GDN_CB, SPAN + 8, LANES), F32),
        ],
        compiler_params=_params(("parallel", "arbitrary")),
        name="gdn_prompt",
    )(proj, ba, conv_wt, alog_row, dtb_row, gnw)


def _head_norm_rope(x, w, cos, sa, sb, seg_lo):
    sq = x * x
    s0 = jnp.sum(jnp.where(seg_lo, sq, 0.0), axis=1, keepdims=True)
    s1 = jnp.sum(jnp.where(seg_lo, 0.0, sq), axis=1, keepdims=True)
    ms = jnp.where(seg_lo, s0, s1) * (1.0 / ATT_HD)
    xn = (x * lax.rsqrt(ms + EPS)) * w
    return xn * cos + pltpu.roll(xn, LANES - ROT_HALF, 1) * sa + pltpu.roll(xn, ROT_HALF, 1) * sb


def _attn_prompt_kernel(q_ref, k_ref, v_ref, cos_ref, sa_ref, sb_ref, qw_ref, kw_ref,
                        o_ref, lse_ref, kprev_ref, vprev_ref, *, dil, jb):
    jg = pl.program_id(1)
    n = pl.program_id(2)
    row = lax.broadcasted_iota(jnp.int32, (SPAN, SPAN), 0)
    col = lax.broadcasted_iota(jnp.int32, (SPAN, SPAN), 1)
    lane = lax.broadcasted_iota(jnp.int32, (SPAN, LANES), 1)
    seg_lo = lane < ATT_HD
    mask_cur = col <= row
    mask_prev = jnp.logical_and(col >= row, n > 0)
    qw = qw_ref[...]
    kw = kw_ref[...]

    @pl.when(n == 0)
    def _():
        kprev_ref[...] = jnp.zeros_like(kprev_ref)
        vprev_ref[...] = jnp.zeros_like(vprev_ref)

    def body(r, carry):
        rows = pl.ds(r, SPAN, stride=dil) if dil > 1 else pl.ds(0, SPAN)
        cos = cos_ref[rows, :]
        sa = sa_ref[rows, :]
        sb = sb_ref[rows, :]
        lse_tile = jnp.zeros((SPAN, LANES), F32)
        for j in range(jb):
            qb = (_head_norm_rope(q_ref[j, rows, :], qw, cos, sa, sb, seg_lo) * (ATT_HD ** -0.5)).astype(BF16)
            kb = _head_norm_rope(k_ref[j, rows, :], kw, cos, sa, sb, seg_lo).astype(BF16)
            vb = v_ref[j, rows, :].astype(BF16)
            kp = kprev_ref[r, :, j * LANES:(j + 1) * LANES]
            vp = vprev_ref[r, :, j * LANES:(j + 1) * LANES]
            o_blk = jnp.zeros((SPAN, LANES), F32)
            for hh in range(2):
                in_head = seg_lo if hh == 0 else jnp.logical_not(seg_lo)
                qh = jnp.where(in_head, qb, jnp.zeros_like(qb))
                s_cur = jnp.where(mask_cur, _dot_nt(qh, kb), NEG)
                s_prev = jnp.where(mask_prev, _dot_nt(qh, kp), NEG)
                mx = jnp.maximum(jnp.max(s_cur, axis=1, keepdims=True), jnp.max(s_prev, axis=1, keepdims=True))
                p_cur = jnp.exp(s_cur - mx)
                p_prev = jnp.exp(s_prev - mx)
                den = jnp.sum(p_cur, axis=1, keepdims=True) + jnp.sum(p_prev, axis=1, keepdims=True)
                acc = _dot(p_cur.astype(BF16), vb) + _dot(p_prev.astype(BF16), vp)
                o_blk = jnp.where(in_head, acc / den, o_blk)
                head = (jg * jb + j) * 2 + hh
                lse_tile = jnp.where(lane == head, mx + jnp.log(den), lse_tile)
            o_ref[j, rows, :] = o_blk
            kprev_ref[r, :, j * LANES:(j + 1) * LANES] = kb
            vprev_ref[r, :, j * LANES:(j + 1) * LANES] = vb
        lse_ref[rows, :] = lse_tile
        return carry

    lax.fori_loop(0, dil, body, 0)


def _attn_prompt(proj, tabs, qw, kw, gi, batch, seq, jb):
    window, dil = ATT_GROUPS[gi]
    sup = SPAN * dil
    nb = seq // sup
    t = batch * seq
    njg = ATT_CB // jb
    base = (GDN_BLOCKS + gi * 3 * ATT_CB) // jb

    def proj_spec(which):
        return pl.BlockSpec((jb, sup, LANES), lambda b, jg, n: (base + which * njg + jg, b * nb + n, 0))

    tab_spec = pl.BlockSpec((sup, LANES), lambda b, jg, n: (n, 0))
    w_spec = pl.BlockSpec((1, LANES), lambda b, jg, n: (0, 0))
    return pl.pallas_call(
        functools.partial(_attn_prompt_kernel, dil=dil, jb=jb),
        grid=(batch, njg, nb),
        in_specs=[proj_spec(0), proj_spec(1), proj_spec(2), tab_spec, tab_spec, tab_spec, w_spec, w_spec],
        out_specs=[
            pl.BlockSpec((jb, sup, LANES), lambda b, jg, n: (jg, b * nb + n, 0)),
            pl.BlockSpec((None, sup, LANES), lambda b, jg, n: (jg, b * nb + n, 0)),
        ],
        out_shape=[
            jax.ShapeDtypeStruct((ATT_CB, t, LANES), F32),
            jax.ShapeDtypeStruct((njg, t, LANES), F32),
        ],
        scratch_shapes=[
            pltpu.VMEM((dil, SPAN, jb * LANES), BF16),
            pltpu.VMEM((dil, SPAN, jb * LANES), BF16),
        ],
        compiler_params=_params(("parallel", "parallel", "arbitrary")),
        name=f"attn_prompt_w{window}",
    )(proj, proj, proj, tabs[0], tabs[1], tabs[2], qw, kw)


def _kv_tail_kernel(k_ref, v_ref, cos_ref, sa_ref, sb_ref, kw_ref, out_ref):
    rows = k_ref.shape[1]
    lane = lax.broadcasted_iota(jnp.int32, (rows, LANES), 1)
    seg_lo = lane < ATT_HD
    for j in range(ATT_CB):
        out_ref[j] = _head_norm_rope(k_ref[j], kw_ref[...], cos_ref[...], sa_ref[...], sb_ref[...], seg_lo)
        out_ref[ATT_CB + j] = v_ref[j]


def _kv_tail(proj, tabs, kw, gi, batch, seq):
    window, _ = ATT_GROUPS[gi]
    wt = min(window, 512)
    nt = window // wt
    per_seq = seq // wt
    base = (GDN_BLOCKS + gi * 3 * ATT_CB) // ATT_CB

    def rowblk(b, i):
        return b * per_seq + per_seq - nt + i

    tab_spec = pl.BlockSpec((wt, LANES), lambda b, i: (per_seq - nt + i, 0))
    return pl.pallas_call(
        _kv_tail_kernel,
        grid=(batch, nt),
        in_specs=[
            pl.BlockSpec((ATT_CB, wt, LANES), lambda b, i: (base + 1, rowblk(b, i), 0)),
            pl.BlockSpec((ATT_CB, wt, LANES), lambda b, i: (base + 2, rowblk(b, i), 0)),
            tab_spec, tab_spec, tab_spec,
            pl.BlockSpec((1, LANES), lambda b, i: (0, 0)),
        ],
        out_specs=pl.BlockSpec((2 * ATT_CB, wt, LANES), lambda b, i: (0, b * nt + i, 0)),
        out_shape=jax.ShapeDtypeStruct((2 * ATT_CB, batch * window, LANES), F32),
        compiler_params=_params(("parallel", "parallel")),
        name=f"kv_tail_w{window}",
    )(proj, proj, tabs[0], tabs[1], tabs[2], kw)


def _route(logits):
    lane = lax.broadcasted_iota(jnp.int32, logits.shape, 1).astype(F32)
    big = float(4 * LANES)
    is_g = jnp.logical_and(lane >= MOE_EXPERTS, lane < MOE_EXPERTS + MOE_GROUPS)
    gl = jnp.where(is_g, logits, NEG)
    gmax = jnp.max(gl, axis=1, keepdims=True)
    gidx = jnp.min(jnp.where(gl == gmax, lane, big), axis=1, keepdims=True) - MOE_EXPERTS
    p_group = 1.0 / jnp.sum(jnp.where(is_g, jnp.exp(gl - gmax), 0.0), axis=1, keepdims=True)
    lo = gidx * MOE_PER_GROUP
    in_grp = jnp.logical_and(lane >= lo, lane < lo + MOE_PER_GROUP)
    el = jnp.where(in_grp, logits, NEG)
    e1 = jnp.max(el, axis=1, keepdims=True)
    i1 = jnp.min(jnp.where(el == e1, lane, big), axis=1, keepdims=True)
    el2 = jnp.where(lane == i1, NEG, el)
    e2 = jnp.max(el2, axis=1, keepdims=True)
    i2 = jnp.min(jnp.where(el2 == e2, lane, big), axis=1, keepdims=True)
    t = jnp.exp(e2 - e1)
    w1 = 1.0 / (1.0 + t)
    w2 = t / (1.0 + t)
    return jnp.where(lane == i1, w1 * p_group, jnp.where(lane == i2, w2 * p_group, 0.0))


def _outproj_kernel(x_ref, oa_ref, o0_ref, l0_ref, o1_ref, l1_ref, o2_ref, l2_ref, wout_ref, n2_ref, wr_ref,
                    h_ref, hn_ref, gate_ref):
    l0 = jnp.sum(l0_ref[...], axis=0)
    l1 = jnp.sum(l1_ref[...], axis=0)
    l2 = jnp.sum(l2_ref[...], axis=0)
    mx = jnp.maximum(jnp.maximum(l0, l1), l2)
    e0 = jnp.exp(l0 - mx)
    e1 = jnp.exp(l1 - mx)
    e2 = jnp.exp(l2 - mx)
    inv = 1.0 / (e0 + e1 + e2)
    hrow = lax.broadcasted_iota(jnp.int32, (LANES, ATT_W), 0)
    hcol = lax.broadcasted_iota(jnp.int32, (LANES, ATT_W), 1)
    expand = jnp.where(lax.shift_right_logical(hcol, 6) == hrow, 1.0, 0.0).astype(BF16)
    w0 = _dot_sel_right(e0 * inv, expand)
    w1 = _dot_sel_right(e1 * inv, expand)
    w2 = _dot_sel_right(e2 * inv, expand)
    acc = x_ref[...]
    for h in range(GDN_HEADS):
        acc = acc + _dot(oa_ref[h].astype(BF16), wout_ref[h * LANES:(h + 1) * LANES, :])
    for j in range(ATT_CB):
        sl = slice(j * LANES, (j + 1) * LANES)
        ob = w0[:, sl] * o0_ref[j] + w1[:, sl] * o1_ref[j] + w2[:, sl] * o2_ref[j]
        acc = acc + _dot(ob.astype(BF16), wout_ref[GDN_HEADS * LANES + j * LANES:GDN_HEADS * LANES + (j + 1) * LANES, :])
    h_ref[...] = acc
    hn = ((acc * lax.rsqrt(jnp.mean(acc * acc, axis=-1, keepdims=True) + EPS)) * n2_ref[...]).astype(BF16)
    hn_ref[...] = hn
    gate_ref[...] = _route(_dot(hn, wr_ref[...]))


def _outproj(x2d, oa, attn, w_out, norm2_w, w_router, tm):
    t = x2d.shape[0]
    specs = [pl.BlockSpec((tm, D_MODEL), lambda i: (i, 0)),
             pl.BlockSpec((GDN_HEADS, tm, LANES), lambda i: (0, i, 0))]
    args = [x2d, oa]
    for o, lse in attn:
        specs.append(pl.BlockSpec((ATT_CB, tm, LANES), lambda i: (0, i, 0)))
        specs.append(pl.BlockSpec((lse.shape[0], tm, LANES), lambda i: (0, i, 0)))
        args += [o, lse]
    specs += [pl.BlockSpec((D_MODEL, D_MODEL), lambda i: (0, 0)),
              pl.BlockSpec((1, D_MODEL), lambda i: (0, 0)),
              pl.BlockSpec((D_MODEL, LANES), lambda i: (0, 0))]
    args += [w_out, norm2_w, w_router]
    return pl.pallas_call(
        _outproj_kernel,
        grid=(t // tm,),
        in_specs=specs,
        out_specs=[pl.BlockSpec((tm, D_MODEL), lambda i: (i, 0)),
                   pl.BlockSpec((tm, D_MODEL), lambda i: (i, 0)),
                   pl.BlockSpec((tm, LANES), lambda i: (i, 0))],
        out_shape=[jax.ShapeDtypeStruct((t, D_MODEL), F32),
                   jax.ShapeDtypeStruct((t, D_MODEL), BF16),
                   jax.ShapeDtypeStruct((t, LANES), F32)],
        compiler_params=_params(("parallel",)),
        name="outproj_router",
    )(*args)


def _moe_kernel(hn_ref, gate_ref, h_ref, wgu_ref, wd_ref, y_ref, acc_ref):
    e = pl.program_id(1)

    @pl.when(e == 0)
    def _():
        acc_ref[...] = jnp.zeros_like(acc_ref)

    gu = _dot(hn_ref[...], wgu_ref[...].astype(BF16))
    hid = _silu(gu[:, :MOE_FF]) * gu[:, MOE_FF:]
    d = _dot(hid.astype(BF16), wd_ref[...].astype(BF16))
    gate = gate_ref[...]
    lane = lax.broadcasted_iota(jnp.int32, gate.shape, 1)
    gcol = jnp.sum(jnp.where(lane == e, gate, 0.0), axis=1, keepdims=True)
    acc_ref[...] += gcol * d

    @pl.when(e == MOE_EXPERTS - 1)
    def _():
        y_ref[...] = h_ref[...] + acc_ref[...]


def _moe(hn, gate, h, w_gate_up, w_down, tm):
    t = hn.shape[0]
    return pl.pallas_call(
        _moe_kernel,
        grid=(t // tm, MOE_EXPERTS),
        in_specs=[
            pl.BlockSpec((tm, D_MODEL), lambda i, e: (i, 0)),
            pl.BlockSpec((tm, LANES), lambda i, e: (i, 0)),
            pl.BlockSpec((tm, D_MODEL), lambda i, e: (i, 0)),
            pl.BlockSpec((None, None, D_MODEL, 2 * MOE_FF), lambda i, e: (0, e, 0, 0)),
            pl.BlockSpec((None, None, MOE_FF, D_MODEL), lambda i, e: (0, e, 0, 0)),
        ],
        out_specs=pl.BlockSpec((tm, D_MODEL), lambda i, e: (i, 0)),
        out_shape=jax.ShapeDtypeStruct((t, D_MODEL), F32),
        scratch_shapes=[pltpu.VMEM((tm, D_MODEL), F32)],
        compiler_params=_params(("parallel", "arbitrary")),
        name="moe",
    )(hn, gate, h, w_gate_up, w_down)


def _gdn_sample_kernel(x_ref, ba_ref, cs_ref, cw_ref, alog_ref, dtb_ref, gnw_ref, s_ref,
                       so_ref, oa_ref, co_ref, qt_ref, kt_ref, v_ref, beta_ref, g_ref):
    b = pl.program_id(0)
    nb = x_ref.shape[1]

    @pl.when(b == 0)
    def _():
        cw = cw_ref[...]
        qkv = []
        for cb in range(GDN_CB):
            sl = slice(cb * LANES, (cb + 1) * LANES)
            u = x_ref[cb]
            acc = cw[3:4, sl] * u
            for tap in range(GDN_CONV - 1):
                acc = acc + cw[tap:tap + 1, sl] * cs_ref[tap, :, sl]
            qkv.append(_silu(acc))
            co_ref[0, :, sl] = cs_ref[1, :, sl]
            co_ref[1, :, sl] = cs_ref[2, :, sl]
            co_ref[2, :, sl] = u
        for h in range(GDN_HEADS):
            qt_ref[h] = jnp.transpose(_l2norm(qkv[h]) * (GDN_D ** -0.5))
            kt_ref[h] = jnp.transpose(_l2norm(qkv[GDN_HEADS + h]))
            v_ref[h] = qkv[2 * GDN_HEADS + h]
        beta_t, g_t = _gates(ba_ref[...], alog_ref[...], dtb_ref[...])
        beta_ref[...] = beta_t
        g_ref[...] = g_t

    lane_is_b = lax.broadcasted_iota(jnp.int32, (GDN_D, nb), 1) == b
    brow = beta_ref[pl.ds(b, 1), :]
    grow = g_ref[pl.ds(b, 1), :]
    for h in range(GDN_HEADS):
        kcol = jnp.sum(jnp.where(lane_is_b, kt_ref[h], 0.0), axis=1, keepdims=True)
        qcol = jnp.sum(jnp.where(lane_is_b, qt_ref[h], 0.0), axis=1, keepdims=True)
        bet = brow[:, h:h + 1]
        eg = jnp.exp(grow[:, GDN_HEADS + h:GDN_HEADS + h + 1])
        vrow = v_ref[h, pl.ds(b, 1), :]
        s = s_ref[h]
        v_new = vrow * bet - jnp.sum((kcol * (bet * eg)) * s, axis=0, keepdims=True)
        intra = jnp.sum(qcol * kcol, axis=0, keepdims=True)
        o = jnp.sum((qcol * eg) * s, axis=0, keepdims=True) + intra * v_new
        so_ref[h] = s * eg + kcol * v_new
        oa_ref[h, pl.ds(b, 1), :] = _gdn_out_norm(o, gnw_ref[...], x_ref[3 * GDN_HEADS + h, pl.ds(b, 1), :])


def _gdn_sample(proj, ba, conv_state, conv_wt, alog_row, dtb_row, gnw, state):
    nb = state.shape[0]
    full = lambda shape: pl.BlockSpec(shape, lambda b: (0,) * len(shape))
    return pl.pallas_call(
        _gdn_sample_kernel,
        grid=(nb,),
        in_specs=[
            full((GDN_BLOCKS, nb, LANES)),
            full((nb, LANES)),
            full((GDN_CONV - 1, nb, GDN_CH)),
            full((GDN_CONV, GDN_CH)),
            full((1, LANES)), full((1, LANES)), full((1, LANES)),
            pl.BlockSpec((None, GDN_HEADS, GDN_D, GDN_D), lambda b: (b, 0, 0, 0)),
        ],
        out_specs=[
            pl.BlockSpec((None, GDN_HEADS, GDN_D, GDN_D), lambda b: (b, 0, 0, 0)),
            full((GDN_HEADS, nb, LANES)),
            full((GDN_CONV - 1, nb, GDN_CH)),
        ],
        out_shape=[
            jax.ShapeDtypeStruct((nb, GDN_HEADS, GDN_D, GDN_D), F32),
            jax.ShapeDtypeStruct((GDN_HEADS, nb, LANES), F32),
            jax.ShapeDtypeStruct((GDN_CONV - 1, nb, GDN_CH), F32),
        ],
        scratch_shapes=[
            pltpu.VMEM((GDN_HEADS, GDN_D, nb), F32),
            pltpu.VMEM((GDN_HEADS, GDN_D, nb), F32),
            pltpu.VMEM((GDN_HEADS, nb, GDN_D), F32),
            pltpu.VMEM((nb, LANES), F32),
            pltpu.VMEM((nb, LANES), F32),
        ],
        compiler_params=_params(("arbitrary",)),
        name="gdn_sample",
    )(proj, ba, conv_state, conv_wt, alog_row, dtb_row, gnw, state)


def _attn_sample_prep_kernel(xq_ref, xk_ref, xv_ref, qw_ref, kw_ref, cos_ref, sin_ref, q_ref, k_ref, v_ref):
    cos = cos_ref[...]
    sin = sin_ref[...]

    def prep(x_ref, w_ref, scale, out):
        for j in range(ATT_CB):
            xt = jnp.transpose(x_ref[j])
            for hh in range(2):
                xh = xt[hh * ATT_HD:(hh + 1) * ATT_HD]
                ms = jnp.mean(xh * xh, axis=0, keepdims=True)
                base = j * LANES + hh * ATT_HD
                xn = (xh * lax.rsqrt(ms + EPS)) * w_ref[base:base + ATT_HD, :]
                x1 = xn[0:ROT_HALF]
                x2 = xn[ROT_HALF:ROT_DIM]
                out[base:base + ROT_HALF, :] = (x1 * cos - x2 * sin) * scale
                out[base + ROT_HALF:base + ROT_DIM, :] = (x2 * cos + x1 * sin) * scale
                out[base + ROT_DIM:base + ATT_HD, :] = xn[ROT_DIM:] * scale

    prep(xq_ref, qw_ref, ATT_HD ** -0.5, q_ref)
    prep(xk_ref, kw_ref, 1.0, k_ref)
    for j in range(ATT_CB):
        v_ref[j * LANES:(j + 1) * LANES, :] = jnp.transpose(xv_ref[j])


def _attn_sample_prep(proj, qw_cols, kw_cols, cos_s, sin_s):
    nb = proj.shape[1]
    grp = lambda g: (g, 0, 0)
    return pl.pallas_call(
        _attn_sample_prep_kernel,
        grid=(3,),
        in_specs=[
            pl.BlockSpec((ATT_CB, nb, LANES), lambda g: (GDN_BLOCKS // ATT_CB + 3 * g, 0, 0)),
            pl.BlockSpec((ATT_CB, nb, LANES), lambda g: (GDN_BLOCKS // ATT_CB + 3 * g + 1, 0, 0)),
            pl.BlockSpec((ATT_CB, nb, LANES), lambda g: (GDN_BLOCKS // ATT_CB + 3 * g + 2, 0, 0)),
            pl.BlockSpec((None, ATT_W, nb), grp),
            pl.BlockSpec((None, ATT_W, nb), grp),
            pl.BlockSpec((ROT_HALF, nb), lambda g: (0, 0)),
            pl.BlockSpec((ROT_HALF, nb), lambda g: (0, 0)),
        ],
        out_specs=[pl.BlockSpec((None, ATT_W, nb), grp)] * 3,
        out_shape=[jax.ShapeDtypeStruct((3, ATT_W, nb), F32)] * 3,
        compiler_params=_params(("parallel",)),
        name="attn_sample_prep",
    )(proj, proj, proj, qw_cols, kw_cols, cos_s, sin_s)


def _attn_sample_kernel(c_ref, q_ref, k_ref, v_ref, co_ref, o_ref, lse_ref, p_ref, ps_ref, *, dil):
    b = pl.program_id(0)
    kv = pl.program_id(1)
    w = c_ref.shape[1]
    nb = q_ref.shape[1]
    lane_b = lax.broadcasted_iota(jnp.int32, (ATT_W, nb), 1) == b
    lane_w = lax.broadcasted_iota(jnp.int32, (ATT_W, w), 1)
    blk = c_ref[...]
    shifted = pltpu.roll(blk, w - 1, 1)

    @pl.when(jnp.logical_and(b == 0, kv == 0))
    def _():
        o_ref[...] = jnp.zeros_like(o_ref)
        lse_ref[...] = jnp.zeros_like(lse_ref)

    @pl.when(kv == 0)
    def _():
        qcol = jnp.sum(jnp.where(lane_b, q_ref[...], 0.0), axis=1, keepdims=True)
        kcol = jnp.sum(jnp.where(lane_b, k_ref[...], 0.0), axis=1, keepdims=True)
        prod = blk * qcol
        self_prod = qcol * kcol
        s = jnp.concatenate([jnp.sum(prod[h * ATT_HD:(h + 1) * ATT_HD], axis=0, keepdims=True)
                             for h in range(ATT_HEADS)], axis=0)
        s_self = jnp.concatenate([jnp.sum(self_prod[h * ATT_HD:(h + 1) * ATT_HD], axis=0, keepdims=True)
                                  for h in range(ATT_HEADS)], axis=0)
        pos = lax.broadcasted_iota(jnp.int32, (ATT_HEADS, w), 1)
        s = jnp.where((pos & (dil - 1)) == 0, s, NEG)
        mx = jnp.maximum(jnp.max(s, axis=1, keepdims=True), s_self)
        p = jnp.exp(s - mx)
        p_self = jnp.exp(s_self - mx)
        den = jnp.sum(p, axis=1, keepdims=True) + p_self
        p_ref[...] = p / den
        ps_ref[...] = jnp.broadcast_to(p_self / den, ps_ref.shape)
        lane8 = lax.broadcasted_iota(jnp.int32, (ATT_HEADS, nb), 1)
        lse_ref[...] = jnp.where(lane8 == b, mx + jnp.log(den), lse_ref[...])
        co_ref[...] = jnp.where(lane_w == w - 1, kcol, shifted)

    @pl.when(kv == 1)
    def _():
        vcol = jnp.sum(jnp.where(lane_b, v_ref[...], 0.0), axis=1, keepdims=True)
        p = p_ref[...]
        ps = ps_ref[...]
        parts = []
        for h in range(ATT_HEADS):
            sl = slice(h * ATT_HD, (h + 1) * ATT_HD)
            parts.append(jnp.sum(blk[sl] * p[h:h + 1, :], axis=1, keepdims=True) + ps[h:h + 1, 0:1] * vcol[sl])
        ocol = jnp.concatenate(parts, axis=0)
        o_ref[...] = jnp.where(lane_b, ocol, o_ref[...])
        co_ref[...] = jnp.where(lane_w == w - 1, vcol, shifted)


def _attn_sample(cache_t, q_t, k_t, v_t, gi):
    window, dil = ATT_GROUPS[gi]
    nb = cache_t.shape[0]
    res = lambda shape: pl.BlockSpec(shape, lambda b, kv: (gi, 0, 0))
    return pl.pallas_call(
        functools.partial(_attn_sample_kernel, dil=dil),
        grid=(nb, 2),
        in_specs=[
            pl.BlockSpec((None, ATT_W, window), lambda b, kv: (b, kv, 0)),
            res((None, ATT_W, nb)), res((None, ATT_W, nb)), res((None, ATT_W, nb)),
        ],
        out_specs=[
            pl.BlockSpec((None, ATT_W, window), lambda b, kv: (b, kv, 0)),
            pl.BlockSpec((ATT_W, nb), lambda b, kv: (0, 0)),
            pl.BlockSpec((ATT_HEADS, nb), lambda b, kv: (0, 0)),
        ],
        out_shape=[
            jax.ShapeDtypeStruct(cache_t.shape, F32),
            jax.ShapeDtypeStruct((ATT_W, nb), F32),
            jax.ShapeDtypeStruct((ATT_HEADS, nb), F32),
        ],
        scratch_shapes=[pltpu.VMEM((ATT_HEADS, window), F32), pltpu.VMEM((ATT_HEADS, LANES), F32)],
        compiler_params=_params(("arbitrary", "arbitrary")),
        name=f"attn_sample_w{window}",
    )(cache_t, q_t, k_t, v_t)


def _rope_tables(pos):
    inv = jnp.exp(-math.log(ROPE_THETA) * jnp.arange(ROT_HALF, dtype=F32) * (2.0 / ROT_DIM))
    ang = pos.astype(F32)[:, None] * inv[None, :]
    return jnp.cos(ang), jnp.sin(ang)


def _rope_lane_tables(pos):
    cos, sin = _rope_tables(pos)
    n = pos.shape[0]
    ones = jnp.ones((n, ATT_HD - ROT_DIM), F32)
    zeros = jnp.zeros((n, ATT_HD - ROT_DIM), F32)
    z8 = jnp.zeros((n, ROT_HALF), F32)
    cos_t = jnp.concatenate([cos, cos, ones], axis=1)
    sa_t = jnp.concatenate([-sin, z8, zeros], axis=1)
    sb_t = jnp.concatenate([z8, sin, zeros], axis=1)
    return tuple(jnp.concatenate([a, a], axis=1) for a in (cos_t, sa_t, sb_t))


def kernel(x_prompt, x_sample, state_gdn, state_conv, cache_kv_w128, cache_kv_w512, cache_kv_w2048,
           norm1_w, w_in, conv_w, a_log, dt_bias, gdn_norm_w, q_norm_w, k_norm_w, w_out, norm2_w,
           w_router_group, w_router_expert, w_gate_up, w_down):
    bp, lp, _ = x_prompt.shape
    nb = x_sample.shape[0]
    assert x_sample.shape[1] == 1 and state_gdn.shape[0] == 1
    caches = (cache_kv_w128, cache_kv_w512, cache_kv_w2048)

    w_t = jnp.transpose(w_in[0])
    n_gdn = GDN_BLOCKS * LANES
    w_main = jnp.transpose(jnp.concatenate([w_t[:n_gdn], w_t[n_gdn + 2 * GDN_HEADS:]], axis=0)).astype(BF16)
    w_ba = jnp.pad(jnp.transpose(w_t[n_gdn:n_gdn + 2 * GDN_HEADS]), ((0, 0), (0, LANES - 2 * GDN_HEADS))).astype(BF16)
    conv_wt = jnp.transpose(conv_w[0])
    pad4 = lambda v: jnp.pad(v, (GDN_HEADS, LANES - 2 * GDN_HEADS))[None, :]
    alog_row = pad4(a_log[0])
    dtb_row = pad4(dt_bias[0])
    gnw = gdn_norm_w[0][None, :]
    n1 = norm1_w[0][None, :]
    n2 = norm2_w[0][None, :]
    w_out_b = w_out[0].astype(BF16)
    w_router = jnp.pad(jnp.concatenate([w_router_expert[0], w_router_group[0]], axis=1),
                       ((0, 0), (0, LANES - MOE_EXPERTS - MOE_GROUPS))).astype(BF16)
    qw_rows = [jnp.tile(q_norm_w[0, g], 2)[None, :] for g in range(3)]
    kw_rows = [jnp.tile(k_norm_w[0, g], 2)[None, :] for g in range(3)]
    tabs_p = _rope_lane_tables(jnp.arange(lp, dtype=jnp.int32))

    xp = x_prompt.reshape(bp * lp, D_MODEL)
    proj_p, ba_p = _inproj(xp, n1, w_main, w_ba, tm=1024)
    oa_p, sg_p, sc_p = _gdn_prompt(proj_p, ba_p, conv_wt, alog_row, dtb_row, gnw, bp, lp)
    attn_p = []
    kv_p = []
    for gi, jb in enumerate((4, 4, 2)):
        attn_p.append(_attn_prompt(proj_p, tabs_p, qw_rows[gi], kw_rows[gi], gi, bp, lp, jb))
        window = ATT_GROUPS[gi][0]
        kvt = _kv_tail(proj_p, tabs_p, kw_rows[gi], gi, bp, lp)
        kv_p.append(jnp.transpose(kvt.reshape(2 * ATT_CB, bp, window, LANES), (1, 2, 0, 3))
                    .reshape(1, bp, window, 2, ATT_HEADS, ATT_HD))
    h_p, hn_p, gate_p = _outproj(xp, oa_p, attn_p, w_out_b, n2, w_router, tm=512)
    y_p = _moe(hn_p, gate_p, h_p, w_gate_up, w_down, tm=1024)

    xs = x_sample.reshape(nb, D_MODEL)
    proj_s, ba_s = _inproj(xs, n1, w_main, w_ba, tm=nb)
    conv_state = jnp.transpose(state_conv[0], (1, 0, 2))
    sg_s, oa_s, conv_new = _gdn_sample(proj_s, ba_s, conv_state, conv_wt, alog_row, dtb_row, gnw, state_gdn[0])
    cos_s, sin_s = _rope_tables(jnp.full((1,), PAST_LEN, jnp.int32))
    cos_s = jnp.broadcast_to(jnp.transpose(cos_s), (ROT_HALF, nb))
    sin_s = jnp.broadcast_to(jnp.transpose(sin_s), (ROT_HALF, nb))
    qw_cols = jnp.broadcast_to(jnp.tile(q_norm_w[0], (1, ATT_HEADS))[:, :, None], (3, ATT_W, nb))
    kw_cols = jnp.broadcast_to(jnp.tile(k_norm_w[0], (1, ATT_HEADS))[:, :, None], (3, ATT_W, nb))
    q_t, k_t, v_t = _attn_sample_prep(proj_s, qw_cols, kw_cols, cos_s, sin_s)
    attn_s = []
    kv_s = []
    for gi in range(3):
        window = ATT_GROUPS[gi][0]
        cache_t = jnp.transpose(caches[gi][0], (0, 2, 3, 4, 1)).reshape(nb, 2 * ATT_W, window)
        new_cache, o_t, lse_t = _attn_sample(cache_t, q_t, k_t, v_t, gi)
        kv_s.append(jnp.transpose(new_cache.reshape(nb, 2, ATT_HEADS, ATT_HD, window), (0, 4, 1, 2, 3))[None])
        o_rows = jnp.transpose(o_t.reshape(ATT_CB, LANES, nb), (0, 2, 1))
        lse_rows = jnp.pad(jnp.transpose(lse_t), ((0, 0), (0, LANES - ATT_HEADS)))[None]
        attn_s.append((o_rows, lse_rows))
    h_s, hn_s, gate_s = _outproj(xs, oa_s, attn_s, w_out_b, n2, w_router, tm=nb)
    y_s = _moe(hn_s, gate_s, h_s, w_gate_up, w_down, tm=nb)

    return (y_p.reshape(bp, lp, D_MODEL), y_s.reshape(nb, 1, D_MODEL),
            sg_p[None], sc_p[None], kv_p[0], kv_p[1], kv_p[2],
            sg_s[None], jnp.transpose(conv_new, (1, 0, 2))[None], kv_s[0], kv_s[1], kv_s[2])
```

```python
import functools
import math

import jax
import jax.numpy as jnp
from jax import lax
from jax.experimental import pallas as pl
from jax.experimental.pallas import tpu as pltpu

F32 = jnp.float32
BF16 = jnp.bfloat16

LANES = 128
D_MODEL = 1024
GDN_HEADS = 4
GDN_D = 128
GDN_CONV = 4
GDN_CH = 3 * GDN_HEADS * GDN_D
GDN_CB = GDN_CH // LANES
GDN_BLOCKS = 16
ATT_GROUPS = ((128, 1), (512, 4), (2048, 16))
ATT_HEADS = 8
ATT_HD = 64
ATT_W = ATT_HEADS * ATT_HD
ATT_CB = ATT_W // LANES
SPAN = 128
ROT_DIM = 16
ROT_HALF = ROT_DIM // 2
ROPE_THETA = 500000.0
MOE_GROUPS = 4
MOE_PER_GROUP = 8
MOE_EXPERTS = 32
MOE_FF = 256
EPS = 1e-6
PAST_LEN = 8192
NEG = -1e30
N_MAIN_BLOCKS = GDN_BLOCKS + 3 * 3 * ATT_CB
VMEM_LIMIT = 48 * 1024 * 1024
SAMPLE_BLOCK_BYTES = 4 * 1024 * 1024


def _dot(a, b):
    return jnp.dot(a, b, preferred_element_type=F32)


def _dot_nt(a, b):
    return lax.dot_general(a, b, (((1,), (1,)), ((), ())), preferred_element_type=F32)


def _bdot(a, b):
    return _dot(a.astype(BF16), b.astype(BF16))


def _split3(x):
    hi = x.astype(BF16)
    r1 = x - hi.astype(F32)
    mid = r1.astype(BF16)
    lo = (r1 - mid.astype(F32)).astype(BF16)
    return hi, mid, lo


def _dot_sel_left(sel_bf16, x):
    hi, mid, lo = _split3(x)
    return _dot(sel_bf16, hi) + _dot(sel_bf16, mid) + _dot(sel_bf16, lo)


def _dot_sel_right(x, sel_bf16):
    hi, mid, lo = _split3(x)
    return _dot(hi, sel_bf16) + _dot(mid, sel_bf16) + _dot(lo, sel_bf16)


def _silu(x):
    return x * jax.nn.sigmoid(x)


def _softplus(x):
    return jnp.maximum(x, 0.0) + jnp.log1p(jnp.exp(-jnp.abs(x)))


def _params(sem, vmem=VMEM_LIMIT):
    return pltpu.CompilerParams(dimension_semantics=sem, vmem_limit_bytes=vmem)


def _inproj_kernel(x_ref, nw_ref, w_ref, wba_ref, out_ref, ba_ref, xn_ref, *, nblk):
    @pl.when(pl.program_id(1) == 0)
    def _():
        x = x_ref[...]
        ms = jnp.mean(x * x, axis=-1, keepdims=True)
        xn = ((x * lax.rsqrt(ms + EPS)) * nw_ref[...]).astype(BF16)
        xn_ref[...] = xn
        ba_ref[...] = _dot(xn, wba_ref[...])

    res = _dot(xn_ref[...], w_ref[...])
    for jj in range(nblk):
        out_ref[jj] = res[:, jj * LANES:(jj + 1) * LANES]


def _inproj(x2d, norm_w, w_main, w_ba, tm, tn=512):
    t = x2d.shape[0]
    ncol = w_main.shape[1]
    nblk = tn // LANES
    return pl.pallas_call(
        functools.partial(_inproj_kernel, nblk=nblk),
        grid=(t // tm, ncol // tn),
        in_specs=[
            pl.BlockSpec((tm, D_MODEL), lambda i, j: (i, 0)),
            pl.BlockSpec((1, D_MODEL), lambda i, j: (0, 0)),
            pl.BlockSpec((D_MODEL, tn), lambda i, j: (0, j)),
            pl.BlockSpec((D_MODEL, LANES), lambda i, j: (0, 0)),
        ],
        out_specs=[
            pl.BlockSpec((nblk, tm, LANES), lambda i, j: (j, i, 0)),
            pl.BlockSpec((tm, LANES), lambda i, j: (i, 0)),
        ],
        out_shape=[
            jax.ShapeDtypeStruct((ncol // LANES, t, LANES), F32),
            jax.ShapeDtypeStruct((t, LANES), F32),
        ],
        scratch_shapes=[pltpu.VMEM((tm, D_MODEL), BF16)],
        compiler_params=_params(("parallel", "arbitrary")),
        name="inproj",
    )(x2d, norm_w, w_main, w_ba)


def _gates(ba, alog_row, dtb_row):
    beta = jax.nn.sigmoid(ba)
    g = -jnp.exp(alog_row) * _softplus(ba + dtb_row)
    return beta, g


def _l2norm(x):
    return x * lax.rsqrt(jnp.sum(x * x, axis=-1, keepdims=True) + EPS)


def _gdn_out_norm(o, gnw, z):
    on = (o * lax.rsqrt(jnp.mean(o * o, axis=-1, keepdims=True) + EPS)) * gnw
    return on * _silu(z)


def _mm3(a, b):
    ah = a.astype(BF16)
    al = (a - ah.astype(F32)).astype(BF16)
    bh = b.astype(BF16)
    bl = (b - bh.astype(F32)).astype(BF16)
    return _dot(ah, bh) + _dot(ah, bl) + _dot(al, bh)


def _gdn_prompt_kernel(x_ref, ba_ref, cw_ref, alog_ref, dtb_ref, gnw_ref,
                       oa_ref, so_ref, co_ref, s_ref, cbuf_ref, *, nchunks, nseq, refine):
    c = pl.program_id(1)
    ch = SPAN

    @pl.when(c == 0)
    def _():
        s_ref[...] = jnp.zeros_like(s_ref)
        cbuf_ref[:, :, 0:8, :] = jnp.zeros((nseq, GDN_CB, 8, LANES), F32)

    cw = cw_ref[...]
    row = lax.broadcasted_iota(jnp.int32, (ch, ch), 0)
    col = lax.broadcasted_iota(jnp.int32, (ch, ch), 1)
    causal = row >= col
    strict = row > col
    tril = jnp.where(causal, 1.0, 0.0).astype(BF16)
    eye = jnp.where(row == col, 1.0, 0.0)

    @pl.when(c == nchunks - 1)
    def _():
        for sq in range(nseq):
            for cb in range(GDN_CB):
                co_ref[sq, :, cb * LANES:(cb + 1) * LANES] = x_ref[cb, sq, ch - 3:ch, :]

    chains = [(sq, h) for sq in range(nseq) for h in range(GDN_HEADS)]
    qs, ks, kbs, rhss, decays, egs, kdecs, elast = [], [], [], [], [], [], [], []
    for sq in range(nseq):
        qkv = []
        for cb in range(GDN_CB):
            cbuf_ref[sq, cb, 8:8 + ch, :] = x_ref[cb, sq]
            acc = None
            for tap in range(GDN_CONV):
                term = cw[tap:tap + 1, cb * LANES:(cb + 1) * LANES] * cbuf_ref[sq, cb, 5 + tap:5 + tap + ch, :]
                acc = term if acc is None else acc + term
            qkv.append(_silu(acc))
            cbuf_ref[sq, cb, 0:8, :] = cbuf_ref[sq, cb, ch:ch + 8, :]

        beta_t, g_t = _gates(ba_ref[sq], alog_ref[...], dtb_ref[...])
        gc = _dot_sel_left(tril, g_t)
        gct = jnp.transpose(gc)
        for h in range(GDN_HEADS):
            gl = GDN_HEADS + h
            gcol = gc[:, gl:gl + 1]
            grow = gct[gl:gl + 1, :]
            bcol = beta_t[:, h:h + 1]
            glast = gc[ch - 1:ch, gl:gl + 1]
            k = _l2norm(qkv[GDN_HEADS + h])
            kb = k * bcol
            eg = jnp.exp(gcol)
            qs.append(_l2norm(qkv[h]) * (GDN_D ** -0.5))
            ks.append(k.astype(BF16))
            kbs.append(kb.astype(BF16))
            rhss.append(jnp.concatenate([qkv[2 * GDN_HEADS + h] * bcol, kb * eg], axis=1))
            decays.append(jnp.exp(jnp.where(causal, gcol - grow, NEG)))
            egs.append(eg)
            kdecs.append(jnp.transpose(k * jnp.exp(glast - gcol)).astype(BF16))
            elast.append(jnp.exp(glast))

    n = len(chains)
    a_low = [jnp.where(strict, _dot_nt(kbs[i], ks[i]) * decays[i], 0.0) for i in range(n)]
    intra = [jnp.where(causal, _dot_nt(qs[i].astype(BF16), ks[i]) * decays[i], 0.0).astype(BF16) for i in range(n)]
    m = [-a for a in a_low]
    p = [eye + mi for mi in m]
    for _ in range(6):
        m = [_bdot(mi, mi) for mi in m]
        p = [pi + _bdot(pi, mi) for pi, mi in zip(p, m)]
    pb = [pi.astype(BF16) for pi in p]
    x = [_dot(pb[i], rhss[i].astype(BF16)) for i in range(n)]
    for _ in range(refine):
        res = [rhss[i] - x[i] - _mm3(a_low[i], x[i]) for i in range(n)]
        x = [x[i] + _dot(pb[i], res[i].astype(BF16)) for i in range(n)]
    sb = [s_ref[sq, h].astype(BF16) for sq, h in chains]
    v_new = [(x[i][:, :GDN_D] - _dot(x[i][:, GDN_D:].astype(BF16), sb[i])).astype(BF16) for i in range(n)]
    o = [_dot((qs[i] * egs[i]).astype(BF16), sb[i]) + _dot(intra[i], v_new[i]) for i in range(n)]
    for i, (sq, h) in enumerate(chains):
        s_ref[sq, h] = s_ref[sq, h] * elast[i] + _dot(kdecs[i], v_new[i])
        oa_ref[h, sq] = _gdn_out_norm(o[i], gnw_ref[...], x_ref[3 * GDN_HEADS + h, sq])

    @pl.when(c == nchunks - 1)
    def _():
        so_ref[...] = s_ref[...]


def _gdn_prompt(proj, ba, conv_wt, alog_row, dtb_row, gnw, batch, seq, nseq, refine=2):
    nchunks = seq // SPAN
    return pl.pallas_call(
        functools.partial(_gdn_prompt_kernel, nchunks=nchunks, nseq=nseq, refine=refine),
        grid=(batch // nseq, nchunks),
        in_specs=[
            pl.BlockSpec((GDN_BLOCKS, nseq, SPAN, LANES), lambda b, c: (0, b, c, 0)),
            pl.BlockSpec((nseq, SPAN, LANES), lambda b, c: (b, c, 0)),
            pl.BlockSpec((GDN_CONV, GDN_CH), lambda b, c: (0, 0)),
            pl.BlockSpec((1, LANES), lambda b, c: (0, 0)),
            pl.BlockSpec((1, LANES), lambda b, c: (0, 0)),
            pl.BlockSpec((1, LANES), lambda b, c: (0, 0)),
        ],
        out_specs=[
            pl.BlockSpec((GDN_HEADS, nseq, SPAN, LANES), lambda b, c: (0, b, c, 0)),
            pl.BlockSpec((nseq, GDN_HEADS, GDN_D, GDN_D), lambda b, c: (b, 0, 0, 0)),
            pl.BlockSpec((nseq, GDN_CONV - 1, GDN_CH), lambda b, c: (b, 0, 0)),
        ],
        out_shape=[
            jax.ShapeDtypeStruct((GDN_HEADS, batch, seq, LANES), F32),
            jax.ShapeDtypeStruct((batch, GDN_HEADS, GDN_D, GDN_D), F32),
            jax.ShapeDtypeStruct((batch, GDN_CONV - 1, GDN_CH), F32),
        ],
        scratch_shapes=[
            pltpu.VMEM((nseq, GDN_HEADS, GDN_D, GDN_D), F32),
            pltpu.VMEM((nseq, GDN_CB, SPAN + 8, LANES), F32),
        ],
        compiler_params=_params(("parallel", "arbitrary")),
        name="gdn_prompt",
    )(proj, ba, conv_wt, alog_row, dtb_row, gnw)


def _head_norm_rope(x, w, cos, sa, sb, seg_lo):
    sq = x * x
    s0 = jnp.sum(jnp.where(seg_lo, sq, 0.0), axis=1, keepdims=True)
    s1 = jnp.sum(jnp.where(seg_lo, 0.0, sq), axis=1, keepdims=True)
    ms = jnp.where(seg_lo, s0, s1) * (1.0 / ATT_HD)
    xn = (x * lax.rsqrt(ms + EPS)) * w
    return xn * cos + pltpu.roll(xn, LANES - ROT_HALF, 1) * sa + pltpu.roll(xn, ROT_HALF, 1) * sb


def _attn_prompt_kernel(q_ref, k_ref, v_ref, cos_ref, sa_ref, sb_ref, qw_ref, kw_ref,
                        o_ref, lse_ref, kprev_ref, vprev_ref, *, dil, jb, ru):
    n = pl.program_id(2)
    row = lax.broadcasted_iota(jnp.int32, (SPAN, 2 * SPAN), 0)
    col = lax.broadcasted_iota(jnp.int32, (SPAN, 2 * SPAN), 1)
    mask = jnp.logical_or(col <= row, jnp.logical_and(col - SPAN >= row, n > 0))
    lane = lax.broadcasted_iota(jnp.int32, (SPAN, LANES), 1)
    seg_lo = lane < ATT_HD
    lane2 = lax.broadcasted_iota(jnp.int32, (2 * SPAN, LANES), 1)
    in_head2 = [lane2 < ATT_HD, lane2 >= ATT_HD]
    er = lax.broadcasted_iota(jnp.int32, (LANES, LANES), 0)
    ec = lax.broadcasted_iota(jnp.int32, (LANES, LANES), 1)
    seg_sum = jnp.where((er < ATT_HD) == (ec < ATT_HD), 1.0, 0.0).astype(BF16)
    qw = qw_ref[...]
    kw = kw_ref[...]

    @pl.when(n == 0)
    def _():
        kprev_ref[...] = jnp.zeros_like(kprev_ref)
        vprev_ref[...] = jnp.zeros_like(vprev_ref)

    def body(it, carry):
        rs = [it * ru + u for u in range(ru)]
        rows = [pl.ds(r, SPAN, stride=dil) if dil > 1 else pl.ds(0, SPAN) for r in rs]
        blocks = [(u, j) for u in range(ru) for j in range(jb)]
        nblk = len(blocks)
        tabs = [(cos_ref[rows[u], :], sa_ref[rows[u], :], sb_ref[rows[u], :]) for u in range(ru)]
        xs = [q_ref[j, rows[u], :] for u, j in blocks] + [k_ref[j, rows[u], :] for u, j in blocks]
        ws = [qw] * nblk + [kw] * nblk
        sq = [x * x for x in xs]
        sq_hi = [s.astype(BF16) for s in sq]
        sq_lo = [(s - h.astype(F32)).astype(BF16) for s, h in zip(sq, sq_hi)]
        ssq = [_dot(h, seg_sum) + _dot(l, seg_sum) for h, l in zip(sq_hi, sq_lo)]
        xn = [(x * lax.rsqrt(s * (1.0 / ATT_HD) + EPS)) * w for x, s, w in zip(xs, ssq, ws)]
        roped = []
        for i, v in enumerate(xn):
            cos, sa, sb = tabs[blocks[i % nblk][0]]
            roped.append(v * cos + pltpu.roll(v, LANES - ROT_HALF, 1) * sa + pltpu.roll(v, ROT_HALF, 1) * sb)
        qb = [(roped[bi] * (ATT_HD ** -0.5)).astype(BF16) for bi in range(nblk)]
        kcat, vcat = [], []
        for bi, (u, j) in enumerate(blocks):
            sl = slice(j * LANES, (j + 1) * LANES)
            kb = roped[nblk + bi].astype(BF16)
            vb = v_ref[j, rows[u], :].astype(BF16)
            kcat.append(jnp.concatenate([kb, kprev_ref[rs[u], :, sl]], axis=0))
            vcat.append(jnp.concatenate([vb, vprev_ref[rs[u], :, sl]], axis=0))
            kprev_ref[rs[u], :, sl] = kb
            vprev_ref[rs[u], :, sl] = vb
        heads = [(bi, hh) for bi in range(nblk) for hh in range(2)]
        in_head = [seg_lo, jnp.logical_not(seg_lo)]
        qh = [jnp.where(in_head[hh], qb[bi], jnp.zeros_like(qb[bi])) for bi, hh in heads]
        s = [jnp.where(mask, _dot_nt(qh[i], kcat[bi]), NEG) for i, (bi, hh) in enumerate(heads)]
        mx = [jnp.max(a, axis=1, keepdims=True) for a in s]
        p = [jnp.exp(a - m).astype(BF16) for a, m in zip(s, mx)]
        ones = jnp.ones((2 * SPAN, LANES), BF16)
        acc = [_dot(p[i], jnp.where(in_head2[hh], vcat[bi], ones)) for i, (bi, hh) in enumerate(heads)]
        for bi, (u, j) in enumerate(blocks):
            a0, a1 = acc[2 * bi], acc[2 * bi + 1]
            den = pltpu.roll(jnp.where(seg_lo, a1, a0), ATT_HD, 1)
            o_ref[j, rows[u], :] = jnp.where(seg_lo, a0, a1) / den
            lse_ref[j, rows[u], :] = jnp.where(seg_lo, mx[2 * bi], mx[2 * bi + 1]) + jnp.log(den)
        return carry

    lax.fori_loop(0, dil // ru, body, 0)


def _attn_prompt(proj, tabs, qw, kw, gi, batch, seq, jb):
    window, dil = ATT_GROUPS[gi]
    sup = SPAN * dil
    nb = seq // sup
    t = batch * seq
    njg = ATT_CB // jb
    ru = min(dil, ATT_CB // jb)
    base = (GDN_BLOCKS + gi * 3 * ATT_CB) // jb

    def proj_spec(which):
        return pl.BlockSpec((jb, sup, LANES), lambda b, jg, n: (base + which * njg + jg, b * nb + n, 0))

    tab_spec = pl.BlockSpec((sup, LANES), lambda b, jg, n: (n, 0))
    w_spec = pl.BlockSpec((1, LANES), lambda b, jg, n: (0, 0))
    return pl.pallas_call(
        functools.partial(_attn_prompt_kernel, dil=dil, jb=jb, ru=ru),
        grid=(batch, njg, nb),
        in_specs=[proj_spec(0), proj_spec(1), proj_spec(2), tab_spec, tab_spec, tab_spec, w_spec, w_spec],
        out_specs=[
            pl.BlockSpec((jb, sup, LANES), lambda b, jg, n: (jg, b * nb + n, 0)),
            pl.BlockSpec((jb, sup, LANES), lambda b, jg, n: (jg, b * nb + n, 0)),
        ],
        out_shape=[
            jax.ShapeDtypeStruct((ATT_CB, t, LANES), F32),
            jax.ShapeDtypeStruct((ATT_CB, t, LANES), F32),
        ],
        scratch_shapes=[
            pltpu.VMEM((dil, SPAN, jb * LANES), BF16),
            pltpu.VMEM((dil, SPAN, jb * LANES), BF16),
        ],
        compiler_params=_params(("parallel", "parallel", "arbitrary")),
        name=f"attn_prompt_w{window}",
    )(proj, proj, proj, tabs[0], tabs[1], tabs[2], qw, kw)


def _kv_tail_kernel(k_ref, v_ref, cos_ref, sa_ref, sb_ref, kw_ref, out_ref):
    rows = k_ref.shape[1]
    lane = lax.broadcasted_iota(jnp.int32, (rows, LANES), 1)
    seg_lo = lane < ATT_HD
    for j in range(ATT_CB):
        out_ref[j] = _head_norm_rope(k_ref[j], kw_ref[...], cos_ref[...], sa_ref[...], sb_ref[...], seg_lo)
        out_ref[ATT_CB + j] = v_ref[j]


def _kv_tail(proj, tabs, kw, gi, batch, seq):
    window, _ = ATT_GROUPS[gi]
    wt = min(window, 512)
    nt = window // wt
    per_seq = seq // wt
    base = (GDN_BLOCKS + gi * 3 * ATT_CB) // ATT_CB

    def rowblk(b, i):
        return b * per_seq + per_seq - nt + i

    tab_spec = pl.BlockSpec((wt, LANES), lambda b, i: (per_seq - nt + i, 0))
    return pl.pallas_call(
        _kv_tail_kernel,
        grid=(batch, nt),
        in_specs=[
            pl.BlockSpec((ATT_CB, wt, LANES), lambda b, i: (base + 1, rowblk(b, i), 0)),
            pl.BlockSpec((ATT_CB, wt, LANES), lambda b, i: (base + 2, rowblk(b, i), 0)),
            tab_spec, tab_spec, tab_spec,
            pl.BlockSpec((1, LANES), lambda b, i: (0, 0)),
        ],
        out_specs=pl.BlockSpec((2 * ATT_CB, wt, LANES), lambda b, i: (0, b * nt + i, 0)),
        out_shape=jax.ShapeDtypeStruct((2 * ATT_CB, batch * window, LANES), F32),
        compiler_params=_params(("parallel", "parallel")),
        name=f"kv_tail_w{window}",
    )(proj, proj, tabs[0], tabs[1], tabs[2], kw)


def _route(logits):
    lane = lax.broadcasted_iota(jnp.int32, logits.shape, 1).astype(F32)
    big = float(4 * LANES)
    is_g = jnp.logical_and(lane >= MOE_EXPERTS, lane < MOE_EXPERTS + MOE_GROUPS)
    gl = jnp.where(is_g, logits, NEG)
    gmax = jnp.max(gl, axis=1, keepdims=True)
    gidx = jnp.min(jnp.where(gl == gmax, lane, big), axis=1, keepdims=True) - MOE_EXPERTS
    p_group = 1.0 / jnp.sum(jnp.where(is_g, jnp.exp(gl - gmax), 0.0), axis=1, keepdims=True)
    lo = gidx * MOE_PER_GROUP
    in_grp = jnp.logical_and(lane >= lo, lane < lo + MOE_PER_GROUP)
    el = jnp.where(in_grp, logits, NEG)
    e1 = jnp.max(el, axis=1, keepdims=True)
    i1 = jnp.min(jnp.where(el == e1, lane, big), axis=1, keepdims=True)
    el2 = jnp.where(lane == i1, NEG, el)
    e2 = jnp.max(el2, axis=1, keepdims=True)
    i2 = jnp.min(jnp.where(el2 == e2, lane, big), axis=1, keepdims=True)
    t = jnp.exp(e2 - e1)
    w1 = 1.0 / (1.0 + t)
    w2 = t / (1.0 + t)
    return jnp.where(lane == i1, w1 * p_group, jnp.where(lane == i2, w2 * p_group, 0.0))


def _outproj_kernel(x_ref, oa_ref, o0_ref, l0_ref, o1_ref, l1_ref, o2_ref, l2_ref, wout_ref, n2_ref, wr_ref,
                    h_ref, hn_ref, gate_ref):
    obs = []
    for j in range(ATT_CB):
        l0, l1, l2 = l0_ref[j], l1_ref[j], l2_ref[j]
        mx = jnp.maximum(jnp.maximum(l0, l1), l2)
        e0 = jnp.exp(l0 - mx)
        e1 = jnp.exp(l1 - mx)
        e2 = jnp.exp(l2 - mx)
        ob = (e0 * o0_ref[j] + e1 * o1_ref[j] + e2 * o2_ref[j]) / (e0 + e1 + e2)
        obs.append(ob.astype(BF16))
    mix = jnp.concatenate([oa_ref[h].astype(BF16) for h in range(GDN_HEADS)] + obs, axis=1)
    acc = x_ref[...] + _dot(mix, wout_ref[...])
    h_ref[...] = acc
    hn = ((acc * lax.rsqrt(jnp.mean(acc * acc, axis=-1, keepdims=True) + EPS)) * n2_ref[...]).astype(BF16)
    hn_ref[...] = hn
    gate_ref[...] = _route(_dot(hn, wr_ref[...]))


def _outproj(x2d, oa, attn, w_out, norm2_w, w_router, tm):
    t = x2d.shape[0]
    specs = [pl.BlockSpec((tm, D_MODEL), lambda i: (i, 0)),
             pl.BlockSpec((GDN_HEADS, tm, LANES), lambda i: (0, i, 0))]
    args = [x2d, oa]
    for o, lse in attn:
        specs.append(pl.BlockSpec((ATT_CB, tm, LANES), lambda i: (0, i, 0)))
        specs.append(pl.BlockSpec((ATT_CB, tm, LANES), lambda i: (0, i, 0)))
        args += [o, lse]
    specs += [pl.BlockSpec((D_MODEL, D_MODEL), lambda i: (0, 0)),
              pl.BlockSpec((1, D_MODEL), lambda i: (0, 0)),
              pl.BlockSpec((D_MODEL, LANES), lambda i: (0, 0))]
    args += [w_out, norm2_w, w_router]
    return pl.pallas_call(
        _outproj_kernel,
        grid=(t // tm,),
        in_specs=specs,
        out_specs=[pl.BlockSpec((tm, D_MODEL), lambda i: (i, 0)),
                   pl.BlockSpec((tm, D_MODEL), lambda i: (i, 0)),
                   pl.BlockSpec((tm, LANES), lambda i: (i, 0))],
        out_shape=[jax.ShapeDtypeStruct((t, D_MODEL), F32),
                   jax.ShapeDtypeStruct((t, D_MODEL), BF16),
                   jax.ShapeDtypeStruct((t, LANES), F32)],
        compiler_params=_params(("parallel",)),
        name="outproj_router",
    )(*args)


def _moe_kernel(hn_ref, gate_ref, h_ref, wgu_ref, wd_ref, y_ref, acc_ref):
    e = pl.program_id(1)

    @pl.when(e == 0)
    def _():
        acc_ref[...] = jnp.zeros_like(acc_ref)

    gu = _dot(hn_ref[...], wgu_ref[...].astype(BF16))
    hid = _silu(gu[:, :MOE_FF]) * gu[:, MOE_FF:]
    d = _dot(hid.astype(BF16), wd_ref[...].astype(BF16))
    gate = gate_ref[...]
    lane = lax.broadcasted_iota(jnp.int32, gate.shape, 1)
    gcol = jnp.sum(jnp.where(lane == e, gate, 0.0), axis=1, keepdims=True)
    acc_ref[...] += gcol * d

    @pl.when(e == MOE_EXPERTS - 1)
    def _():
        y_ref[...] = h_ref[...] + acc_ref[...]


def _moe(hn, gate, h, w_gate_up, w_down, tm):
    t = hn.shape[0]
    return pl.pallas_call(
        _moe_kernel,
        grid=(t // tm, MOE_EXPERTS),
        in_specs=[
            pl.BlockSpec((tm, D_MODEL), lambda i, e: (i, 0)),
            pl.BlockSpec((tm, LANES), lambda i, e: (i, 0)),
            pl.BlockSpec((tm, D_MODEL), lambda i, e: (i, 0)),
            pl.BlockSpec((None, None, D_MODEL, 2 * MOE_FF), lambda i, e: (0, e, 0, 0)),
            pl.BlockSpec((None, None, MOE_FF, D_MODEL), lambda i, e: (0, e, 0, 0)),
        ],
        out_specs=pl.BlockSpec((tm, D_MODEL), lambda i, e: (i, 0)),
        out_shape=jax.ShapeDtypeStruct((t, D_MODEL), F32),
        scratch_shapes=[pltpu.VMEM((tm, D_MODEL), F32)],
        compiler_params=_params(("parallel", "arbitrary")),
        name="moe",
    )(hn, gate, h, w_gate_up, w_down)


def _gdn_sample_kernel(x_ref, ba_ref, cs_ref, cw_ref, alog_ref, dtb_ref, gnw_ref, s_ref,
                       so_ref, oa_ref, co_ref, qt_ref, kt_ref, v_ref, beta_ref, g_ref):
    b = pl.program_id(0)
    nb = x_ref.shape[1]

    @pl.when(b == 0)
    def _():
        cw = cw_ref[...]
        qkv = []
        for cb in range(GDN_CB):
            sl = slice(cb * LANES, (cb + 1) * LANES)
            u = x_ref[cb]
            acc = cw[3:4, sl] * u
            for tap in range(GDN_CONV - 1):
                acc = acc + cw[tap:tap + 1, sl] * cs_ref[tap, :, sl]
            qkv.append(_silu(acc))
            co_ref[0, :, sl] = cs_ref[1, :, sl]
            co_ref[1, :, sl] = cs_ref[2, :, sl]
            co_ref[2, :, sl] = u
        for h in range(GDN_HEADS):
            qt_ref[h] = jnp.transpose(_l2norm(qkv[h]) * (GDN_D ** -0.5))
            kt_ref[h] = jnp.transpose(_l2norm(qkv[GDN_HEADS + h]))
            v_ref[h] = qkv[2 * GDN_HEADS + h]
        beta_t, g_t = _gates(ba_ref[...], alog_ref[...], dtb_ref[...])
        beta_ref[...] = beta_t
        g_ref[...] = g_t

    lane_is_b = lax.broadcasted_iota(jnp.int32, (GDN_D, nb), 1) == b
    brow = beta_ref[pl.ds(b, 1), :]
    grow = g_ref[pl.ds(b, 1), :]
    for h in range(GDN_HEADS):
        kcol = jnp.sum(jnp.where(lane_is_b, kt_ref[h], 0.0), axis=1, keepdims=True)
        qcol = jnp.sum(jnp.where(lane_is_b, qt_ref[h], 0.0), axis=1, keepdims=True)
        bet = brow[:, h:h + 1]
        eg = jnp.exp(grow[:, GDN_HEADS + h:GDN_HEADS + h + 1])
        vrow = v_ref[h, pl.ds(b, 1), :]
        s = s_ref[h]
        v_new = vrow * bet - jnp.sum((kcol * (bet * eg)) * s, axis=0, keepdims=True)
        intra = jnp.sum(qcol * kcol, axis=0, keepdims=True)
        o = jnp.sum((qcol * eg) * s, axis=0, keepdims=True) + intra * v_new
        so_ref[h] = s * eg + kcol * v_new
        oa_ref[h, pl.ds(b, 1), :] = _gdn_out_norm(o, gnw_ref[...], x_ref[3 * GDN_HEADS + h, pl.ds(b, 1), :])


def _gdn_sample(proj, ba, conv_state, conv_wt, alog_row, dtb_row, gnw, state):
    nb = state.shape[0]
    full = lambda shape: pl.BlockSpec(shape, lambda b: (0,) * len(shape))
    return pl.pallas_call(
        _gdn_sample_kernel,
        grid=(nb,),
        in_specs=[
            full((GDN_BLOCKS, nb, LANES)),
            full((nb, LANES)),
            full((GDN_CONV - 1, nb, GDN_CH)),
            full((GDN_CONV, GDN_CH)),
            full((1, LANES)), full((1, LANES)), full((1, LANES)),
            pl.BlockSpec((None, GDN_HEADS, GDN_D, GDN_D), lambda b: (b, 0, 0, 0)),
        ],
        out_specs=[
            pl.BlockSpec((None, GDN_HEADS, GDN_D, GDN_D), lambda b: (b, 0, 0, 0)),
            full((GDN_HEADS, nb, LANES)),
            full((GDN_CONV - 1, nb, GDN_CH)),
        ],
        out_shape=[
            jax.ShapeDtypeStruct((nb, GDN_HEADS, GDN_D, GDN_D), F32),
            jax.ShapeDtypeStruct((GDN_HEADS, nb, LANES), F32),
            jax.ShapeDtypeStruct((GDN_CONV - 1, nb, GDN_CH), F32),
        ],
        scratch_shapes=[
            pltpu.VMEM((GDN_HEADS, GDN_D, nb), F32),
            pltpu.VMEM((GDN_HEADS, GDN_D, nb), F32),
            pltpu.VMEM((GDN_HEADS, nb, GDN_D), F32),
            pltpu.VMEM((nb, LANES), F32),
            pltpu.VMEM((nb, LANES), F32),
        ],
        compiler_params=_params(("arbitrary",)),
        name="gdn_sample",
    )(proj, ba, conv_state, conv_wt, alog_row, dtb_row, gnw, state)


def _attn_sample_prep_kernel(xq_ref, xk_ref, xv_ref, qw_ref, kw_ref, cos_ref, sin_ref, q_ref, k_ref, v_ref):
    cos = cos_ref[...]
    sin = sin_ref[...]

    def prep(x_ref, w_ref, scale, out):
        for j in range(ATT_CB):
            xt = jnp.transpose(x_ref[j])
            for hh in range(2):
                xh = xt[hh * ATT_HD:(hh + 1) * ATT_HD]
                ms = jnp.mean(xh * xh, axis=0, keepdims=True)
                base = j * LANES + hh * ATT_HD
                xn = (xh * lax.rsqrt(ms + EPS)) * w_ref[base:base + ATT_HD, :]
                x1 = xn[0:ROT_HALF]
                x2 = xn[ROT_HALF:ROT_DIM]
                out[base:base + ROT_HALF, :] = (x1 * cos - x2 * sin) * scale
                out[base + ROT_HALF:base + ROT_DIM, :] = (x2 * cos + x1 * sin) * scale
                out[base + ROT_DIM:base + ATT_HD, :] = xn[ROT_DIM:] * scale

    prep(xq_ref, qw_ref, ATT_HD ** -0.5, q_ref)
    prep(xk_ref, kw_ref, 1.0, k_ref)
    for j in range(ATT_CB):
        v_ref[j * LANES:(j + 1) * LANES, :] = jnp.transpose(xv_ref[j])


def _attn_sample_prep(proj, qw_cols, kw_cols, cos_s, sin_s):
    nb = proj.shape[1]
    grp = lambda g: (g, 0, 0)
    return pl.pallas_call(
        _attn_sample_prep_kernel,
        grid=(3,),
        in_specs=[
            pl.BlockSpec((ATT_CB, nb, LANES), lambda g: (GDN_BLOCKS // ATT_CB + 3 * g, 0, 0)),
            pl.BlockSpec((ATT_CB, nb, LANES), lambda g: (GDN_BLOCKS // ATT_CB + 3 * g + 1, 0, 0)),
            pl.BlockSpec((ATT_CB, nb, LANES), lambda g: (GDN_BLOCKS // ATT_CB + 3 * g + 2, 0, 0)),
            pl.BlockSpec((None, ATT_W, nb), grp),
            pl.BlockSpec((None, ATT_W, nb), grp),
            pl.BlockSpec((ROT_HALF, nb), lambda g: (0, 0)),
            pl.BlockSpec((ROT_HALF, nb), lambda g: (0, 0)),
        ],
        out_specs=[pl.BlockSpec((None, ATT_W, nb), grp)] * 3,
        out_shape=[jax.ShapeDtypeStruct((3, ATT_W, nb), F32)] * 3,
        compiler_params=_params(("parallel",)),
        name="attn_sample_prep",
    )(proj, proj, proj, qw_cols, kw_cols, cos_s, sin_s)


def _attn_sample_kernel(c_ref, q_ref, k_ref, v_ref, co_ref, o_ref, lse_ref, p_ref, ps_ref, *, dil, bt):
    i = pl.program_id(0)
    kv = pl.program_id(1)
    w = c_ref.shape[2]
    nb = q_ref.shape[1]
    lane_w = lax.broadcasted_iota(jnp.int32, (ATT_W, w), 1)
    lane_nb = lax.broadcasted_iota(jnp.int32, (ATT_W, nb), 1)

    def shift_in(blk, new_col):
        return jnp.where(lane_w == w - 1, new_col, pltpu.roll(blk, w - 1, 1))

    @pl.when(jnp.logical_and(i == 0, kv == 0))
    def _():
        o_ref[...] = jnp.zeros_like(o_ref)
        lse_ref[...] = jnp.zeros_like(lse_ref)

    def k_step(t, carry):
        b = i * bt + t
        lane_b = lane_nb == b
        blk = c_ref[t]
        qcol = jnp.sum(jnp.where(lane_b, q_ref[...], 0.0), axis=1, keepdims=True)
        kcol = jnp.sum(jnp.where(lane_b, k_ref[...], 0.0), axis=1, keepdims=True)
        prod = blk * qcol
        self_prod = qcol * kcol
        s = jnp.concatenate([jnp.sum(prod[h * ATT_HD:(h + 1) * ATT_HD], axis=0, keepdims=True)
                             for h in range(ATT_HEADS)], axis=0)
        s_self = jnp.concatenate([jnp.sum(self_prod[h * ATT_HD:(h + 1) * ATT_HD], axis=0, keepdims=True)
                                  for h in range(ATT_HEADS)], axis=0)
        pos = lax.broadcasted_iota(jnp.int32, (ATT_HEADS, w), 1)
        s = jnp.where((pos & (dil - 1)) == 0, s, NEG)
        mx = jnp.maximum(jnp.max(s, axis=1, keepdims=True), s_self)
        p = jnp.exp(s - mx)
        p_self = jnp.exp(s_self - mx)
        den = jnp.sum(p, axis=1, keepdims=True) + p_self
        p_ref[t] = p / den
        ps_ref[t] = jnp.broadcast_to(p_self / den, (ATT_HEADS, LANES))
        lane8 = lax.broadcasted_iota(jnp.int32, (ATT_HEADS, nb), 1)
        lse_ref[...] = jnp.where(lane8 == b, mx + jnp.log(den), lse_ref[...])
        co_ref[t] = shift_in(blk, kcol)
        return carry

    def v_step(t, carry):
        b = i * bt + t
        lane_b = lane_nb == b
        blk = c_ref[t]
        vcol = jnp.sum(jnp.where(lane_b, v_ref[...], 0.0), axis=1, keepdims=True)
        p = p_ref[t]
        ps = ps_ref[t]
        parts = []
        for h in range(ATT_HEADS):
            sl = slice(h * ATT_HD, (h + 1) * ATT_HD)
            parts.append(jnp.sum(blk[sl] * p[h:h + 1, :], axis=1, keepdims=True) + ps[h:h + 1, 0:1] * vcol[sl])
        ocol = jnp.concatenate(parts, axis=0)
        o_ref[...] = jnp.where(lane_b, ocol, o_ref[...])
        co_ref[t] = shift_in(blk, vcol)
        return carry

    @pl.when(kv == 0)
    def _():
        lax.fori_loop(0, bt, k_step, 0)

    @pl.when(kv == 1)
    def _():
        lax.fori_loop(0, bt, v_step, 0)


def _attn_sample(cache_t, q_t, k_t, v_t, gi, bt):
    window, dil = ATT_GROUPS[gi]
    nb = cache_t.shape[0]
    res = lambda shape: pl.BlockSpec(shape, lambda i, kv: (gi, 0, 0))
    return pl.pallas_call(
        functools.partial(_attn_sample_kernel, dil=dil, bt=bt),
        grid=(nb // bt, 2),
        in_specs=[
            pl.BlockSpec((bt, ATT_W, window), lambda i, kv: (i, kv, 0)),
            res((None, ATT_W, nb)), res((None, ATT_W, nb)), res((None, ATT_W, nb)),
        ],
        out_specs=[
            pl.BlockSpec((bt, ATT_W, window), lambda i, kv: (i, kv, 0)),
            pl.BlockSpec((ATT_W, nb), lambda i, kv: (0, 0)),
            pl.BlockSpec((ATT_HEADS, nb), lambda i, kv: (0, 0)),
        ],
        out_shape=[
            jax.ShapeDtypeStruct(cache_t.shape, F32),
            jax.ShapeDtypeStruct((ATT_W, nb), F32),
            jax.ShapeDtypeStruct((ATT_HEADS, nb), F32),
        ],
        scratch_shapes=[pltpu.VMEM((bt, ATT_HEADS, window), F32), pltpu.VMEM((bt, ATT_HEADS, LANES), F32)],
        compiler_params=_params(("arbitrary", "arbitrary")),
        name=f"attn_sample_w{window}",
    )(cache_t, q_t, k_t, v_t)


def _rope_tables(pos):
    inv = jnp.exp(-math.log(ROPE_THETA) * jnp.arange(ROT_HALF, dtype=F32) * (2.0 / ROT_DIM))
    ang = pos.astype(F32)[:, None] * inv[None, :]
    return jnp.cos(ang), jnp.sin(ang)


def _rope_lane_tables(pos):
    cos, sin = _rope_tables(pos)
    n = pos.shape[0]
    ones = jnp.ones((n, ATT_HD - ROT_DIM), F32)
    zeros = jnp.zeros((n, ATT_HD - ROT_DIM), F32)
    z8 = jnp.zeros((n, ROT_HALF), F32)
    cos_t = jnp.concatenate([cos, cos, ones], axis=1)
    sa_t = jnp.concatenate([-sin, z8, zeros], axis=1)
    sb_t = jnp.concatenate([z8, sin, zeros], axis=1)
    return tuple(jnp.concatenate([a, a], axis=1) for a in (cos_t, sa_t, sb_t))


def kernel(x_prompt, x_sample, state_gdn, state_conv, cache_kv_w128, cache_kv_w512, cache_kv_w2048,
           norm1_w, w_in, conv_w, a_log, dt_bias, gdn_norm_w, q_norm_w, k_norm_w, w_out, norm2_w,
           w_router_group, w_router_expert, w_gate_up, w_down):
    bp, lp, _ = x_prompt.shape
    nb = x_sample.shape[0]
    assert x_sample.shape[1] == 1 and state_gdn.shape[0] == 1
    caches = (cache_kv_w128, cache_kv_w512, cache_kv_w2048)

    w_t = jnp.transpose(w_in[0])
    n_gdn = GDN_BLOCKS * LANES
    w_main = jnp.transpose(jnp.concatenate([w_t[:n_gdn], w_t[n_gdn + 2 * GDN_HEADS:]], axis=0)).astype(BF16)
    w_ba = jnp.pad(jnp.transpose(w_t[n_gdn:n_gdn + 2 * GDN_HEADS]), ((0, 0), (0, LANES - 2 * GDN_HEADS))).astype(BF16)
    conv_wt = jnp.transpose(conv_w[0])
    pad4 = lambda v: jnp.pad(v, (GDN_HEADS, LANES - 2 * GDN_HEADS))[None, :]
    alog_row = pad4(a_log[0])
    dtb_row = pad4(dt_bias[0])
    gnw = gdn_norm_w[0][None, :]
    n1 = norm1_w[0][None, :]
    n2 = norm2_w[0][None, :]
    w_out_b = w_out[0].astype(BF16)
    w_router = jnp.pad(jnp.concatenate([w_router_expert[0], w_router_group[0]], axis=1),
                       ((0, 0), (0, LANES - MOE_EXPERTS - MOE_GROUPS))).astype(BF16)
    qw_rows = [jnp.tile(q_norm_w[0, g], 2)[None, :] for g in range(3)]
    kw_rows = [jnp.tile(k_norm_w[0, g], 2)[None, :] for g in range(3)]
    tabs_p = _rope_lane_tables(jnp.arange(lp, dtype=jnp.int32))

    xp = x_prompt.reshape(bp * lp, D_MODEL)
    proj_p, ba_p = _inproj(xp, n1, w_main, w_ba, tm=2048)
    oa_p, sg_p, sc_p = _gdn_prompt(proj_p.reshape(N_MAIN_BLOCKS, bp, lp, LANES), ba_p.reshape(bp, lp, LANES),
                                   conv_wt, alog_row, dtb_row, gnw, bp, lp, nseq=math.gcd(bp, 4))
    oa_p = oa_p.reshape(GDN_HEADS, bp * lp, LANES)
    attn_p = []
    kv_p = []
    for gi, jb in enumerate((4, 4, 2)):
        attn_p.append(_attn_prompt(proj_p, tabs_p, qw_rows[gi], kw_rows[gi], gi, bp, lp, jb))
        window = ATT_GROUPS[gi][0]
        kvt = _kv_tail(proj_p, tabs_p, kw_rows[gi], gi, bp, lp)
        kv_p.append(jnp.transpose(kvt.reshape(2 * ATT_CB, bp, window, LANES), (1, 2, 0, 3))
                    .reshape(1, bp, window, 2, ATT_HEADS, ATT_HD))
    h_p, hn_p, gate_p = _outproj(xp, oa_p, attn_p, w_out_b, n2, w_router, tm=512)
    y_p = _moe(hn_p, gate_p, h_p, w_gate_up, w_down, tm=1024)

    xs = x_sample.reshape(nb, D_MODEL)
    proj_s, ba_s = _inproj(xs, n1, w_main, w_ba, tm=nb)
    conv_state = jnp.transpose(state_conv[0], (1, 0, 2))
    sg_s, oa_s, conv_new = _gdn_sample(proj_s, ba_s, conv_state, conv_wt, alog_row, dtb_row, gnw, state_gdn[0])
    cos_s, sin_s = _rope_tables(jnp.full((1,), PAST_LEN, jnp.int32))
    cos_s = jnp.broadcast_to(jnp.transpose(cos_s), (ROT_HALF, nb))
    sin_s = jnp.broadcast_to(jnp.transpose(sin_s), (ROT_HALF, nb))
    qw_cols = jnp.broadcast_to(jnp.tile(q_norm_w[0], (1, ATT_HEADS))[:, :, None], (3, ATT_W, nb))
    kw_cols = jnp.broadcast_to(jnp.tile(k_norm_w[0], (1, ATT_HEADS))[:, :, None], (3, ATT_W, nb))
    q_t, k_t, v_t = _attn_sample_prep(proj_s, qw_cols, kw_cols, cos_s, sin_s)
    attn_s = []
    kv_s = []
    for gi in range(3):
        window = ATT_GROUPS[gi][0]
        cache_t = jnp.transpose(caches[gi][0], (0, 2, 3, 4, 1)).reshape(nb, 2 * ATT_W, window)
        bt = math.gcd(nb, max(1, SAMPLE_BLOCK_BYTES // (ATT_W * window * 4)))
        new_cache, o_t, lse_t = _attn_sample(cache_t, q_t, k_t, v_t, gi, bt)
        kv_s.append(jnp.transpose(new_cache.reshape(nb, 2, ATT_HEADS, ATT_HD, window), (0, 4, 1, 2, 3))[None])
        o_rows = jnp.transpose(o_t.reshape(ATT_CB, LANES, nb), (0, 2, 1))
        lse_rows = jnp.transpose(jnp.repeat(jnp.transpose(lse_t), ATT_HD, axis=1).reshape(nb, ATT_CB, LANES), (1, 0, 2))
        attn_s.append((o_rows, lse_rows))
    h_s, hn_s, gate_s = _outproj(xs, oa_s, attn_s, w_out_b, n2, w_router, tm=nb)
    y_s = _moe(hn_s, gate_s, h_s, w_gate_up, w_down, tm=nb)

    return (y_p.reshape(bp, lp, D_MODEL), y_s.reshape(nb, 1, D_MODEL),
            sg_p[None], sc_p[None], kv_p[0], kv_p[1], kv_p[2],
            sg_s[None], jnp.transpose(conv_new, (1, 0, 2))[None], kv_s[0], kv_s[1], kv_s[2])
```

```python
import functools
import math

import jax
import jax.numpy as jnp
from jax import lax
from jax.experimental import pallas as pl
from jax.experimental.pallas import tpu as pltpu

F32 = jnp.float32
BF16 = jnp.bfloat16

LANES = 128
D_MODEL = 1024
GDN_HEADS = 4
GDN_D = 128
GDN_CONV = 4
GDN_CH = 3 * GDN_HEADS * GDN_D
GDN_CB = GDN_CH // LANES
GDN_BLOCKS = 16
ATT_GROUPS = ((128, 1), (512, 4), (2048, 16))
ATT_HEADS = 8
ATT_HD = 64
ATT_W = ATT_HEADS * ATT_HD
ATT_CB = ATT_W // LANES
SPAN = 128
ROT_DIM = 16
ROT_HALF = ROT_DIM // 2
ROPE_THETA = 500000.0
MOE_GROUPS = 4
MOE_PER_GROUP = 8
MOE_EXPERTS = 32
MOE_FF = 256
EPS = 1e-6
PAST_LEN = 8192
NEG = -1e30
N_MAIN_BLOCKS = GDN_BLOCKS + 3 * 3 * ATT_CB
VMEM_LIMIT = 48 * 1024 * 1024
SAMPLE_BLOCK_BYTES = 8 * 1024 * 1024


def _dot(a, b):
    return jnp.dot(a, b, preferred_element_type=F32)


def _dot_nt(a, b):
    return lax.dot_general(a, b, (((1,), (1,)), ((), ())), preferred_element_type=F32)


def _bdot(a, b):
    return _dot(a.astype(BF16), b.astype(BF16))


def _split3(x):
    hi = x.astype(BF16)
    r1 = x - hi.astype(F32)
    mid = r1.astype(BF16)
    lo = (r1 - mid.astype(F32)).astype(BF16)
    return hi, mid, lo


def _dot_sel_left(sel_bf16, x):
    hi, mid, lo = _split3(x)
    return _dot(sel_bf16, hi) + _dot(sel_bf16, mid) + _dot(sel_bf16, lo)


def _dot_sel_right(x, sel_bf16):
    hi, mid, lo = _split3(x)
    return _dot(hi, sel_bf16) + _dot(mid, sel_bf16) + _dot(lo, sel_bf16)


def _silu(x):
    return x * jax.nn.sigmoid(x)


def _softplus(x):
    return jnp.maximum(x, 0.0) + jnp.log1p(jnp.exp(-jnp.abs(x)))


def _params(sem, vmem=VMEM_LIMIT):
    return pltpu.CompilerParams(dimension_semantics=sem, vmem_limit_bytes=vmem)


def _inproj_kernel(x_ref, nw_ref, w_ref, wba_ref, out_ref, ba_ref, xn_ref, *, nblk):
    @pl.when(pl.program_id(1) == 0)
    def _():
        x = x_ref[...]
        ms = jnp.mean(x * x, axis=-1, keepdims=True)
        xn = ((x * lax.rsqrt(ms + EPS)) * nw_ref[...]).astype(BF16)
        xn_ref[...] = xn
        ba_ref[...] = _dot(xn, wba_ref[...])

    res = _dot(xn_ref[...], w_ref[...])
    for jj in range(nblk):
        out_ref[jj] = res[:, jj * LANES:(jj + 1) * LANES]


def _inproj(x2d, norm_w, w_main, w_ba, tm, tn=512):
    t = x2d.shape[0]
    ncol = w_main.shape[1]
    nblk = tn // LANES
    return pl.pallas_call(
        functools.partial(_inproj_kernel, nblk=nblk),
        grid=(t // tm, ncol // tn),
        in_specs=[
            pl.BlockSpec((tm, D_MODEL), lambda i, j: (i, 0)),
            pl.BlockSpec((1, D_MODEL), lambda i, j: (0, 0)),
            pl.BlockSpec((D_MODEL, tn), lambda i, j: (0, j)),
            pl.BlockSpec((D_MODEL, LANES), lambda i, j: (0, 0)),
        ],
        out_specs=[
            pl.BlockSpec((nblk, tm, LANES), lambda i, j: (j, i, 0)),
            pl.BlockSpec((tm, LANES), lambda i, j: (i, 0)),
        ],
        out_shape=[
            jax.ShapeDtypeStruct((ncol // LANES, t, LANES), F32),
            jax.ShapeDtypeStruct((t, LANES), F32),
        ],
        scratch_shapes=[pltpu.VMEM((tm, D_MODEL), BF16)],
        compiler_params=_params(("parallel", "arbitrary")),
        name="inproj",
    )(x2d, norm_w, w_main, w_ba)


def _gates(ba, alog_row, dtb_row):
    beta = jax.nn.sigmoid(ba)
    g = -jnp.exp(alog_row) * _softplus(ba + dtb_row)
    return beta, g


def _l2norm(x):
    return x * lax.rsqrt(jnp.sum(x * x, axis=-1, keepdims=True) + EPS)


def _gdn_out_norm(o, gnw, z):
    on = (o * lax.rsqrt(jnp.mean(o * o, axis=-1, keepdims=True) + EPS)) * gnw
    return on * _silu(z)


def _mm3(a, b):
    ah = a.astype(BF16)
    al = (a - ah.astype(F32)).astype(BF16)
    bh = b.astype(BF16)
    bl = (b - bh.astype(F32)).astype(BF16)
    return _dot(ah, bh) + _dot(ah, bl) + _dot(al, bh)


def _gdn_prompt_kernel(x_ref, ba_ref, cw_ref, alog_ref, dtb_ref, gnw_ref,
                       oa_ref, so_ref, co_ref, s_ref, cbuf_ref, *, nchunks, nseq, refine):
    c = pl.program_id(1)
    ch = SPAN

    @pl.when(c == 0)
    def _():
        s_ref[...] = jnp.zeros_like(s_ref)
        cbuf_ref[:, :, 0:8, :] = jnp.zeros((nseq, GDN_CB, 8, LANES), F32)

    cw = cw_ref[...]
    row = lax.broadcasted_iota(jnp.int32, (ch, ch), 0)
    col = lax.broadcasted_iota(jnp.int32, (ch, ch), 1)
    causal = row >= col
    strict = row > col
    tril = jnp.where(causal, 1.0, 0.0).astype(BF16)
    eye = jnp.where(row == col, 1.0, 0.0)

    @pl.when(c == nchunks - 1)
    def _():
        for sq in range(nseq):
            for cb in range(GDN_CB):
                co_ref[sq, :, cb * LANES:(cb + 1) * LANES] = x_ref[cb, sq, ch - 3:ch, :]

    chains = [(sq, h) for sq in range(nseq) for h in range(GDN_HEADS)]
    qs, ks, kbs, rhss, decays, egs, kdecs, elast = [], [], [], [], [], [], [], []
    for sq in range(nseq):
        qkv = []
        for cb in range(GDN_CB):
            cbuf_ref[sq, cb, 8:8 + ch, :] = x_ref[cb, sq]
            acc = None
            for tap in range(GDN_CONV):
                term = cw[tap:tap + 1, cb * LANES:(cb + 1) * LANES] * cbuf_ref[sq, cb, 5 + tap:5 + tap + ch, :]
                acc = term if acc is None else acc + term
            qkv.append(_silu(acc))
            cbuf_ref[sq, cb, 0:8, :] = cbuf_ref[sq, cb, ch:ch + 8, :]

        beta_t, g_t = _gates(ba_ref[sq], alog_ref[...], dtb_ref[...])
        gc = _dot_sel_left(tril, g_t)
        gct = jnp.transpose(gc)
        for h in range(GDN_HEADS):
            gl = GDN_HEADS + h
            gcol = gc[:, gl:gl + 1]
            grow = gct[gl:gl + 1, :]
            bcol = beta_t[:, h:h + 1]
            glast = gc[ch - 1:ch, gl:gl + 1]
            k = _l2norm(qkv[GDN_HEADS + h])
            kb = k * bcol
            eg = jnp.exp(gcol)
            qs.append(_l2norm(qkv[h]) * (GDN_D ** -0.5))
            ks.append(k.astype(BF16))
            kbs.append(kb.astype(BF16))
            rhss.append(jnp.concatenate([qkv[2 * GDN_HEADS + h] * bcol, kb * eg], axis=1))
            decays.append(jnp.exp(jnp.where(causal, gcol - grow, NEG)))
            egs.append(eg)
            kdecs.append(jnp.transpose(k * jnp.exp(glast - gcol)).astype(BF16))
            elast.append(jnp.exp(glast))

    n = len(chains)
    a_low = [jnp.where(strict, _dot_nt(kbs[i], ks[i]) * decays[i], 0.0) for i in range(n)]
    intra = [jnp.where(causal, _dot_nt(qs[i].astype(BF16), ks[i]) * decays[i], 0.0).astype(BF16) for i in range(n)]
    m = [-a for a in a_low]
    p = [eye + mi for mi in m]
    for _ in range(6):
        m = [_bdot(mi, mi) for mi in m]
        p = [pi + _bdot(pi, mi) for pi, mi in zip(p, m)]
    pb = [pi.astype(BF16) for pi in p]
    x = [_dot(pb[i], rhss[i].astype(BF16)) for i in range(n)]
    for _ in range(refine):
        res = [rhss[i] - x[i] - _mm3(a_low[i], x[i]) for i in range(n)]
        x = [x[i] + _dot(pb[i], res[i].astype(BF16)) for i in range(n)]
    sb = [s_ref[sq, h].astype(BF16) for sq, h in chains]
    v_new = [(x[i][:, :GDN_D] - _dot(x[i][:, GDN_D:].astype(BF16), sb[i])).astype(BF16) for i in range(n)]
    o = [_dot((qs[i] * egs[i]).astype(BF16), sb[i]) + _dot(intra[i], v_new[i]) for i in range(n)]
    for i, (sq, h) in enumerate(chains):
        s_ref[sq, h] = s_ref[sq, h] * elast[i] + _dot(kdecs[i], v_new[i])
        oa_ref[h, sq] = _gdn_out_norm(o[i], gnw_ref[...], x_ref[3 * GDN_HEADS + h, sq])

    @pl.when(c == nchunks - 1)
    def _():
        so_ref[...] = s_ref[...]


def _gdn_prompt(proj, ba, conv_wt, alog_row, dtb_row, gnw, batch, seq, nseq, refine=2):
    nchunks = seq // SPAN
    return pl.pallas_call(
        functools.partial(_gdn_prompt_kernel, nchunks=nchunks, nseq=nseq, refine=refine),
        grid=(batch // nseq, nchunks),
        in_specs=[
            pl.BlockSpec((GDN_BLOCKS, nseq, SPAN, LANES), lambda b, c: (0, b, c, 0)),
            pl.BlockSpec((nseq, SPAN, LANES), lambda b, c: (b, c, 0)),
            pl.BlockSpec((GDN_CONV, GDN_CH), lambda b, c: (0, 0)),
            pl.BlockSpec((1, LANES), lambda b, c: (0, 0)),
            pl.BlockSpec((1, LANES), lambda b, c: (0, 0)),
            pl.BlockSpec((1, LANES), lambda b, c: (0, 0)),
        ],
        out_specs=[
            pl.BlockSpec((GDN_HEADS, nseq, SPAN, LANES), lambda b, c: (0, b, c, 0)),
            pl.BlockSpec((nseq, GDN_HEADS, GDN_D, GDN_D), lambda b, c: (b, 0, 0, 0)),
            pl.BlockSpec((nseq, GDN_CONV - 1, GDN_CH), lambda b, c: (b, 0, 0)),
        ],
        out_shape=[
            jax.ShapeDtypeStruct((GDN_HEADS, batch, seq, LANES), F32),
            jax.ShapeDtypeStruct((batch, GDN_HEADS, GDN_D, GDN_D), F32),
            jax.ShapeDtypeStruct((batch, GDN_CONV - 1, GDN_CH), F32),
        ],
        scratch_shapes=[
            pltpu.VMEM((nseq, GDN_HEADS, GDN_D, GDN_D), F32),
            pltpu.VMEM((nseq, GDN_CB, SPAN + 8, LANES), F32),
        ],
        compiler_params=_params(("parallel", "arbitrary")),
        name="gdn_prompt",
    )(proj, ba, conv_wt, alog_row, dtb_row, gnw)


def _head_norm_rope(x, w, cos, sa, sb, seg_lo):
    sq = x * x
    s0 = jnp.sum(jnp.where(seg_lo, sq, 0.0), axis=1, keepdims=True)
    s1 = jnp.sum(jnp.where(seg_lo, 0.0, sq), axis=1, keepdims=True)
    ms = jnp.where(seg_lo, s0, s1) * (1.0 / ATT_HD)
    xn = (x * lax.rsqrt(ms + EPS)) * w
    return xn * cos + pltpu.roll(xn, LANES - ROT_HALF, 1) * sa + pltpu.roll(xn, ROT_HALF, 1) * sb


def _attn_prompt_kernel(q_ref, k_ref, v_ref, cos_ref, sa_ref, sb_ref, qw_ref, kw_ref,
                        o_ref, lse_ref, kprev_ref, vprev_ref, *, dil, jb, ru):
    n = pl.program_id(2)
    row = lax.broadcasted_iota(jnp.int32, (SPAN, 2 * SPAN), 0)
    col = lax.broadcasted_iota(jnp.int32, (SPAN, 2 * SPAN), 1)
    mask = jnp.logical_or(col <= row, jnp.logical_and(col - SPAN >= row, n > 0))
    lane = lax.broadcasted_iota(jnp.int32, (SPAN, LANES), 1)
    seg_lo = lane < ATT_HD
    lane2 = lax.broadcasted_iota(jnp.int32, (2 * SPAN, LANES), 1)
    in_head2 = [lane2 < ATT_HD, lane2 >= ATT_HD]
    er = lax.broadcasted_iota(jnp.int32, (LANES, LANES), 0)
    ec = lax.broadcasted_iota(jnp.int32, (LANES, LANES), 1)
    seg_sum = jnp.where((er < ATT_HD) == (ec < ATT_HD), 1.0, 0.0).astype(BF16)
    qw = qw_ref[...]
    kw = kw_ref[...]

    @pl.when(n == 0)
    def _():
        kprev_ref[...] = jnp.zeros_like(kprev_ref)
        vprev_ref[...] = jnp.zeros_like(vprev_ref)

    def body(it, carry):
        rs = [it * ru + u for u in range(ru)]
        rows = [pl.ds(r, SPAN, stride=dil) if dil > 1 else pl.ds(0, SPAN) for r in rs]
        blocks = [(u, j) for u in range(ru) for j in range(jb)]
        nblk = len(blocks)
        tabs = [(cos_ref[rows[u], :], sa_ref[rows[u], :], sb_ref[rows[u], :]) for u in range(ru)]
        xs = [q_ref[j, rows[u], :] for u, j in blocks] + [k_ref[j, rows[u], :] for u, j in blocks]
        ws = [qw] * nblk + [kw] * nblk
        sq = [x * x for x in xs]
        sq_hi = [s.astype(BF16) for s in sq]
        sq_lo = [(s - h.astype(F32)).astype(BF16) for s, h in zip(sq, sq_hi)]
        ssq = [_dot(h, seg_sum) + _dot(l, seg_sum) for h, l in zip(sq_hi, sq_lo)]
        xn = [(x * lax.rsqrt(s * (1.0 / ATT_HD) + EPS)) * w for x, s, w in zip(xs, ssq, ws)]
        roped = []
        for i, v in enumerate(xn):
            cos, sa, sb = tabs[blocks[i % nblk][0]]
            roped.append(v * cos + pltpu.roll(v, LANES - ROT_HALF, 1) * sa + pltpu.roll(v, ROT_HALF, 1) * sb)
        qb = [(roped[bi] * (ATT_HD ** -0.5)).astype(BF16) for bi in range(nblk)]
        kcat, vcat = [], []
        for bi, (u, j) in enumerate(blocks):
            sl = slice(j * LANES, (j + 1) * LANES)
            kb = roped[nblk + bi].astype(BF16)
            vb = v_ref[j, rows[u], :].astype(BF16)
            kcat.append(jnp.concatenate([kb, kprev_ref[rs[u], :, sl]], axis=0))
            vcat.append(jnp.concatenate([vb, vprev_ref[rs[u], :, sl]], axis=0))
            kprev_ref[rs[u], :, sl] = kb
            vprev_ref[rs[u], :, sl] = vb
        heads = [(bi, hh) for bi in range(nblk) for hh in range(2)]
        in_head = [seg_lo, jnp.logical_not(seg_lo)]
        qh = [jnp.where(in_head[hh], qb[bi], jnp.zeros_like(qb[bi])) for bi, hh in heads]
        s = [jnp.where(mask, _dot_nt(qh[i], kcat[bi]), NEG) for i, (bi, hh) in enumerate(heads)]
        mx = [jnp.max(a, axis=1, keepdims=True) for a in s]
        p = [jnp.exp(a - m).astype(BF16) for a, m in zip(s, mx)]
        ones = jnp.ones((2 * SPAN, LANES), BF16)
        acc = [_dot(p[i], jnp.where(in_head2[hh], vcat[bi], ones)) for i, (bi, hh) in enumerate(heads)]
        for bi, (u, j) in enumerate(blocks):
            a0, a1 = acc[2 * bi], acc[2 * bi + 1]
            den = pltpu.roll(jnp.where(seg_lo, a1, a0), ATT_HD, 1)
            o_ref[j, rows[u], :] = jnp.where(seg_lo, a0, a1) / den
            lse_ref[j, rows[u], :] = jnp.where(seg_lo, mx[2 * bi], mx[2 * bi + 1]) + jnp.log(den)
        return carry

    lax.fori_loop(0, dil // ru, body, 0)


def _attn_prompt(proj, tabs, qw, kw, gi, batch, seq, jb):
    window, dil = ATT_GROUPS[gi]
    sup = SPAN * dil
    nb = seq // sup
    t = batch * seq
    njg = ATT_CB // jb
    ru = min(dil, ATT_CB // jb)
    base = (GDN_BLOCKS + gi * 3 * ATT_CB) // jb

    def proj_spec(which):
        return pl.BlockSpec((jb, sup, LANES), lambda b, jg, n: (base + which * njg + jg, b * nb + n, 0))

    tab_spec = pl.BlockSpec((sup, LANES), lambda b, jg, n: (n, 0))
    w_spec = pl.BlockSpec((1, LANES), lambda b, jg, n: (0, 0))
    return pl.pallas_call(
        functools.partial(_attn_prompt_kernel, dil=dil, jb=jb, ru=ru),
        grid=(batch, njg, nb),
        in_specs=[proj_spec(0), proj_spec(1), proj_spec(2), tab_spec, tab_spec, tab_spec, w_spec, w_spec],
        out_specs=[
            pl.BlockSpec((jb, sup, LANES), lambda b, jg, n: (jg, b * nb + n, 0)),
            pl.BlockSpec((jb, sup, LANES), lambda b, jg, n: (jg, b * nb + n, 0)),
        ],
        out_shape=[
            jax.ShapeDtypeStruct((ATT_CB, t, LANES), F32),
            jax.ShapeDtypeStruct((ATT_CB, t, LANES), F32),
        ],
        scratch_shapes=[
            pltpu.VMEM((dil, SPAN, jb * LANES), BF16),
            pltpu.VMEM((dil, SPAN, jb * LANES), BF16),
        ],
        compiler_params=_params(("parallel", "parallel", "arbitrary")),
        name=f"attn_prompt_w{window}",
    )(proj, proj, proj, tabs[0], tabs[1], tabs[2], qw, kw)


def _kv_tail_kernel(k_ref, v_ref, cos_ref, sa_ref, sb_ref, kw_ref, out_ref):
    rows = k_ref.shape[1]
    lane = lax.broadcasted_iota(jnp.int32, (rows, LANES), 1)
    seg_lo = lane < ATT_HD
    for j in range(ATT_CB):
        out_ref[j] = _head_norm_rope(k_ref[j], kw_ref[...], cos_ref[...], sa_ref[...], sb_ref[...], seg_lo)
        out_ref[ATT_CB + j] = v_ref[j]


def _kv_tail(proj, tabs, kw, gi, batch, seq):
    window, _ = ATT_GROUPS[gi]
    wt = min(window, 512)
    nt = window // wt
    per_seq = seq // wt
    base = (GDN_BLOCKS + gi * 3 * ATT_CB) // ATT_CB

    def rowblk(b, i):
        return b * per_seq + per_seq - nt + i

    tab_spec = pl.BlockSpec((wt, LANES), lambda b, i: (per_seq - nt + i, 0))
    return pl.pallas_call(
        _kv_tail_kernel,
        grid=(batch, nt),
        in_specs=[
            pl.BlockSpec((ATT_CB, wt, LANES), lambda b, i: (base + 1, rowblk(b, i), 0)),
            pl.BlockSpec((ATT_CB, wt, LANES), lambda b, i: (base + 2, rowblk(b, i), 0)),
            tab_spec, tab_spec, tab_spec,
            pl.BlockSpec((1, LANES), lambda b, i: (0, 0)),
        ],
        out_specs=pl.BlockSpec((2 * ATT_CB, wt, LANES), lambda b, i: (0, b * nt + i, 0)),
        out_shape=jax.ShapeDtypeStruct((2 * ATT_CB, batch * window, LANES), F32),
        compiler_params=_params(("parallel", "parallel")),
        name=f"kv_tail_w{window}",
    )(proj, proj, tabs[0], tabs[1], tabs[2], kw)


def _route(logits):
    lane = lax.broadcasted_iota(jnp.int32, logits.shape, 1).astype(F32)
    big = float(4 * LANES)
    is_g = jnp.logical_and(lane >= MOE_EXPERTS, lane < MOE_EXPERTS + MOE_GROUPS)
    gl = jnp.where(is_g, logits, NEG)
    gmax = jnp.max(gl, axis=1, keepdims=True)
    gidx = jnp.min(jnp.where(gl == gmax, lane, big), axis=1, keepdims=True) - MOE_EXPERTS
    p_group = 1.0 / jnp.sum(jnp.where(is_g, jnp.exp(gl - gmax), 0.0), axis=1, keepdims=True)
    lo = gidx * MOE_PER_GROUP
    in_grp = jnp.logical_and(lane >= lo, lane < lo + MOE_PER_GROUP)
    el = jnp.where(in_grp, logits, NEG)
    e1 = jnp.max(el, axis=1, keepdims=True)
    i1 = jnp.min(jnp.where(el == e1, lane, big), axis=1, keepdims=True)
    el2 = jnp.where(lane == i1, NEG, el)
    e2 = jnp.max(el2, axis=1, keepdims=True)
    i2 = jnp.min(jnp.where(el2 == e2, lane, big), axis=1, keepdims=True)
    t = jnp.exp(e2 - e1)
    w1 = 1.0 / (1.0 + t)
    w2 = t / (1.0 + t)
    return jnp.where(lane == i1, w1 * p_group, jnp.where(lane == i2, w2 * p_group, 0.0))


def _outproj_kernel(x_ref, oa_ref, o0_ref, l0_ref, o1_ref, l1_ref, o2_ref, l2_ref, wout_ref, n2_ref, wr_ref,
                    h_ref, hn_ref, gate_ref):
    obs = []
    for j in range(ATT_CB):
        l0, l1, l2 = l0_ref[j], l1_ref[j], l2_ref[j]
        mx = jnp.maximum(jnp.maximum(l0, l1), l2)
        e0 = jnp.exp(l0 - mx)
        e1 = jnp.exp(l1 - mx)
        e2 = jnp.exp(l2 - mx)
        ob = (e0 * o0_ref[j] + e1 * o1_ref[j] + e2 * o2_ref[j]) / (e0 + e1 + e2)
        obs.append(ob.astype(BF16))
    mix = jnp.concatenate([oa_ref[h].astype(BF16) for h in range(GDN_HEADS)] + obs, axis=1)
    acc = x_ref[...] + _dot(mix, wout_ref[...])
    h_ref[...] = acc
    hn = ((acc * lax.rsqrt(jnp.mean(acc * acc, axis=-1, keepdims=True) + EPS)) * n2_ref[...]).astype(BF16)
    hn_ref[...] = hn
    gate_ref[...] = _route(_dot(hn, wr_ref[...]))


def _outproj(x2d, oa, attn, w_out, norm2_w, w_router, tm):
    t = x2d.shape[0]
    specs = [pl.BlockSpec((tm, D_MODEL), lambda i: (i, 0)),
             pl.BlockSpec((GDN_HEADS, tm, LANES), lambda i: (0, i, 0))]
    args = [x2d, oa]
    for o, lse in attn:
        specs.append(pl.BlockSpec((ATT_CB, tm, LANES), lambda i: (0, i, 0)))
        specs.append(pl.BlockSpec((ATT_CB, tm, LANES), lambda i: (0, i, 0)))
        args += [o, lse]
    specs += [pl.BlockSpec((D_MODEL, D_MODEL), lambda i: (0, 0)),
              pl.BlockSpec((1, D_MODEL), lambda i: (0, 0)),
              pl.BlockSpec((D_MODEL, LANES), lambda i: (0, 0))]
    args += [w_out, norm2_w, w_router]
    return pl.pallas_call(
        _outproj_kernel,
        grid=(t // tm,),
        in_specs=specs,
        out_specs=[pl.BlockSpec((tm, D_MODEL), lambda i: (i, 0)),
                   pl.BlockSpec((tm, D_MODEL), lambda i: (i, 0)),
                   pl.BlockSpec((tm, LANES), lambda i: (i, 0))],
        out_shape=[jax.ShapeDtypeStruct((t, D_MODEL), F32),
                   jax.ShapeDtypeStruct((t, D_MODEL), BF16),
                   jax.ShapeDtypeStruct((t, LANES), F32)],
        compiler_params=_params(("parallel",)),
        name="outproj_router",
    )(*args)


def _moe_kernel(hn_ref, gate_ref, h_ref, wgu_ref, wd_ref, y_ref, acc_ref, *, eb, rsub):
    ei = pl.program_id(1)
    tm = hn_ref.shape[0]

    @pl.when(ei == 0)
    def _():
        acc_ref[...] = jnp.zeros_like(acc_ref)

    units = [(k, r) for k in range(eb) for r in range(tm // rsub)]

    def first_half(k, r):
        rows = slice(r * rsub, (r + 1) * rsub)
        return _dot(hn_ref[rows, :], wgu_ref[k])

    def second_half(k, r, gu):
        rows = slice(r * rsub, (r + 1) * rsub)
        hid = _silu(gu[:, :MOE_FF]) * gu[:, MOE_FF:]
        d = _dot(hid.astype(BF16), wd_ref[k])
        gate = gate_ref[rows, :]
        lane = lax.broadcasted_iota(jnp.int32, gate.shape, 1)
        gcol = jnp.sum(jnp.where(lane == ei * eb + k, gate, 0.0), axis=1, keepdims=True)
        acc_ref[rows, :] += gcol * d

    pending = None
    for k, r in units:
        gu = first_half(k, r)
        if pending is not None:
            second_half(*pending)
        pending = (k, r, gu)
    second_half(*pending)

    @pl.when(ei == pl.num_programs(1) - 1)
    def _():
        y_ref[...] = h_ref[...] + acc_ref[...]


def _moe(hn, gate, h, wgu_b, wd_b, tm, eb=4):
    t = hn.shape[0]
    return pl.pallas_call(
        functools.partial(_moe_kernel, eb=eb, rsub=min(tm, 256)),
        grid=(t // tm, MOE_EXPERTS // eb),
        in_specs=[
            pl.BlockSpec((tm, D_MODEL), lambda i, e: (i, 0)),
            pl.BlockSpec((tm, LANES), lambda i, e: (i, 0)),
            pl.BlockSpec((tm, D_MODEL), lambda i, e: (i, 0)),
            pl.BlockSpec((eb, D_MODEL, 2 * MOE_FF), lambda i, e: (e, 0, 0)),
            pl.BlockSpec((eb, MOE_FF, D_MODEL), lambda i, e: (e, 0, 0)),
        ],
        out_specs=pl.BlockSpec((tm, D_MODEL), lambda i, e: (i, 0)),
        out_shape=jax.ShapeDtypeStruct((t, D_MODEL), F32),
        scratch_shapes=[pltpu.VMEM((tm, D_MODEL), F32)],
        compiler_params=_params(("parallel", "arbitrary")),
        name="moe",
    )(hn, gate, h, wgu_b, wd_b)


def _gdn_sample_kernel(x_ref, ba_ref, cs_ref, cw_ref, alog_ref, dtb_ref, gnw_ref, s_ref,
                       so_ref, oa_ref, co_ref, qt_ref, kt_ref, v_ref, beta_ref, g_ref, *, bt):
    b = pl.program_id(0)
    nb = x_ref.shape[1]

    @pl.when(b == 0)
    def _():
        cw = cw_ref[...]
        qkv = []
        for cb in range(GDN_CB):
            sl = slice(cb * LANES, (cb + 1) * LANES)
            u = x_ref[cb]
            acc = cw[3:4, sl] * u
            for tap in range(GDN_CONV - 1):
                acc = acc + cw[tap:tap + 1, sl] * cs_ref[tap, :, sl]
            qkv.append(_silu(acc))
            co_ref[0, :, sl] = cs_ref[1, :, sl]
            co_ref[1, :, sl] = cs_ref[2, :, sl]
            co_ref[2, :, sl] = u
        for h in range(GDN_HEADS):
            qt_ref[h] = jnp.transpose(_l2norm(qkv[h]) * (GDN_D ** -0.5))
            kt_ref[h] = jnp.transpose(_l2norm(qkv[GDN_HEADS + h]))
            v_ref[h] = qkv[2 * GDN_HEADS + h]
        beta_t, g_t = _gates(ba_ref[...], alog_ref[...], dtb_ref[...])
        beta_ref[...] = beta_t
        g_ref[...] = g_t

    chains = [(t, h) for t in range(bt) for h in range(GDN_HEADS)]
    lane_nb = lax.broadcasted_iota(jnp.int32, (GDN_D, nb), 1)
    is_b = [lane_nb == b * bt + t for t in range(bt)]
    brow = [beta_ref[pl.ds(b * bt + t, 1), :] for t in range(bt)]
    grow = [g_ref[pl.ds(b * bt + t, 1), :] for t in range(bt)]
    kcol = [jnp.sum(jnp.where(is_b[t], kt_ref[h], 0.0), axis=1, keepdims=True) for t, h in chains]
    qcol = [jnp.sum(jnp.where(is_b[t], qt_ref[h], 0.0), axis=1, keepdims=True) for t, h in chains]
    bet = [brow[t][:, h:h + 1] for t, h in chains]
    eg = [jnp.exp(grow[t][:, GDN_HEADS + h:GDN_HEADS + h + 1]) for t, h in chains]
    n = len(chains)
    ws = [jnp.sum((kcol[i] * (bet[i] * eg[i])) * s_ref[t, h], axis=0, keepdims=True) for i, (t, h) in enumerate(chains)]
    qs = [jnp.sum((qcol[i] * eg[i]) * s_ref[t, h], axis=0, keepdims=True) for i, (t, h) in enumerate(chains)]
    intra = [jnp.sum(qcol[i] * kcol[i], axis=0, keepdims=True) for i in range(n)]
    v_new = [v_ref[h, pl.ds(b * bt + t, 1), :] * bet[i] - ws[i] for i, (t, h) in enumerate(chains)]
    outs = [_gdn_out_norm(qs[i] + intra[i] * v_new[i], gnw_ref[...], x_ref[3 * GDN_HEADS + h, pl.ds(b * bt + t, 1), :])
            for i, (t, h) in enumerate(chains)]
    for i, (t, h) in enumerate(chains):
        so_ref[t, h] = s_ref[t, h] * eg[i] + kcol[i] * v_new[i]
        oa_ref[h, pl.ds(b * bt + t, 1), :] = outs[i]


def _gdn_sample(proj, ba, conv_state, conv_wt, alog_row, dtb_row, gnw, state, bt=8):
    nb = state.shape[0]
    full = lambda shape: pl.BlockSpec(shape, lambda b: (0,) * len(shape))
    return pl.pallas_call(
        functools.partial(_gdn_sample_kernel, bt=bt),
        grid=(nb // bt,),
        in_specs=[
            full((GDN_BLOCKS, nb, LANES)),
            full((nb, LANES)),
            full((GDN_CONV - 1, nb, GDN_CH)),
            full((GDN_CONV, GDN_CH)),
            full((1, LANES)), full((1, LANES)), full((1, LANES)),
            pl.BlockSpec((bt, GDN_HEADS, GDN_D, GDN_D), lambda b: (b, 0, 0, 0)),
        ],
        out_specs=[
            pl.BlockSpec((bt, GDN_HEADS, GDN_D, GDN_D), lambda b: (b, 0, 0, 0)),
            full((GDN_HEADS, nb, LANES)),
            full((GDN_CONV - 1, nb, GDN_CH)),
        ],
        out_shape=[
            jax.ShapeDtypeStruct((nb, GDN_HEADS, GDN_D, GDN_D), F32),
            jax.ShapeDtypeStruct((GDN_HEADS, nb, LANES), F32),
            jax.ShapeDtypeStruct((GDN_CONV - 1, nb, GDN_CH), F32),
        ],
        scratch_shapes=[
            pltpu.VMEM((GDN_HEADS, GDN_D, nb), F32),
            pltpu.VMEM((GDN_HEADS, GDN_D, nb), F32),
            pltpu.VMEM((GDN_HEADS, nb, GDN_D), F32),
            pltpu.VMEM((nb, LANES), F32),
            pltpu.VMEM((nb, LANES), F32),
        ],
        compiler_params=_params(("arbitrary",)),
        name="gdn_sample",
    )(proj, ba, conv_state, conv_wt, alog_row, dtb_row, gnw, state)


def _attn_sample_prep_kernel(xq_ref, xk_ref, xv_ref, qw_ref, kw_ref, cos_ref, sin_ref, q_ref, k_ref, v_ref):
    cos = cos_ref[...]
    sin = sin_ref[...]

    def prep(x_ref, w_ref, scale, out):
        for j in range(ATT_CB):
            xt = jnp.transpose(x_ref[j])
            for hh in range(2):
                xh = xt[hh * ATT_HD:(hh + 1) * ATT_HD]
                ms = jnp.mean(xh * xh, axis=0, keepdims=True)
                base = j * LANES + hh * ATT_HD
                xn = (xh * lax.rsqrt(ms + EPS)) * w_ref[base:base + ATT_HD, :]
                x1 = xn[0:ROT_HALF]
                x2 = xn[ROT_HALF:ROT_DIM]
                out[base:base + ROT_HALF, :] = (x1 * cos - x2 * sin) * scale
                out[base + ROT_HALF:base + ROT_DIM, :] = (x2 * cos + x1 * sin) * scale
                out[base + ROT_DIM:base + ATT_HD, :] = xn[ROT_DIM:] * scale

    prep(xq_ref, qw_ref, ATT_HD ** -0.5, q_ref)
    prep(xk_ref, kw_ref, 1.0, k_ref)
    for j in range(ATT_CB):
        v_ref[j * LANES:(j + 1) * LANES, :] = jnp.transpose(xv_ref[j])


def _attn_sample_prep(proj, qw_cols, kw_cols, cos_s, sin_s):
    nb = proj.shape[1]
    grp = lambda g: (g, 0, 0)
    return pl.pallas_call(
        _attn_sample_prep_kernel,
        grid=(3,),
        in_specs=[
            pl.BlockSpec((ATT_CB, nb, LANES), lambda g: (GDN_BLOCKS // ATT_CB + 3 * g, 0, 0)),
            pl.BlockSpec((ATT_CB, nb, LANES), lambda g: (GDN_BLOCKS // ATT_CB + 3 * g + 1, 0, 0)),
            pl.BlockSpec((ATT_CB, nb, LANES), lambda g: (GDN_BLOCKS // ATT_CB + 3 * g + 2, 0, 0)),
            pl.BlockSpec((None, ATT_W, nb), grp),
            pl.BlockSpec((None, ATT_W, nb), grp),
            pl.BlockSpec((ROT_HALF, nb), lambda g: (0, 0)),
            pl.BlockSpec((ROT_HALF, nb), lambda g: (0, 0)),
        ],
        out_specs=[pl.BlockSpec((None, ATT_W, nb), grp)] * 3,
        out_shape=[jax.ShapeDtypeStruct((3, ATT_W, nb), F32)] * 3,
        compiler_params=_params(("parallel",)),
        name="attn_sample_prep",
    )(proj, proj, proj, qw_cols, kw_cols, cos_s, sin_s)


def _attn_sample_kernel(c_ref, q_ref, k_ref, v_ref, co_ref, o_ref, lse_ref, *, dil, bt):
    i = pl.program_id(0)
    w = c_ref.shape[2]
    nb = q_ref.shape[1]
    lane_w = lax.broadcasted_iota(jnp.int32, (ATT_W, w), 1)
    lane_nb = lax.broadcasted_iota(jnp.int32, (ATT_W, nb), 1)

    def shift_in(blk, new_col):
        return jnp.where(lane_w == w - 1, new_col, pltpu.roll(blk, w - 1, 1))

    @pl.when(i == 0)
    def _():
        o_ref[...] = jnp.zeros_like(o_ref)
        lse_ref[...] = jnp.zeros_like(lse_ref)

    def step(t, carry):
        b = i * bt + t
        lane_b = lane_nb == b
        kblk = c_ref[t, 0:ATT_W, :]
        qcol = jnp.sum(jnp.where(lane_b, q_ref[...], 0.0), axis=1, keepdims=True)
        kcol = jnp.sum(jnp.where(lane_b, k_ref[...], 0.0), axis=1, keepdims=True)
        vcol = jnp.sum(jnp.where(lane_b, v_ref[...], 0.0), axis=1, keepdims=True)
        prod = kblk * qcol
        self_prod = qcol * kcol
        s = jnp.concatenate([jnp.sum(prod[h * ATT_HD:(h + 1) * ATT_HD], axis=0, keepdims=True)
                             for h in range(ATT_HEADS)], axis=0)
        s_self = jnp.concatenate([jnp.sum(self_prod[h * ATT_HD:(h + 1) * ATT_HD], axis=0, keepdims=True)
                                  for h in range(ATT_HEADS)], axis=0)
        pos = lax.broadcasted_iota(jnp.int32, (ATT_HEADS, w), 1)
        s = jnp.where((pos & (dil - 1)) == 0, s, NEG)
        mx = jnp.maximum(jnp.max(s, axis=1, keepdims=True), s_self)
        p = jnp.exp(s - mx)
        p_self = jnp.exp(s_self - mx)
        den = jnp.sum(p, axis=1, keepdims=True) + p_self
        p = p / den
        p_self = p_self / den
        lane8 = lax.broadcasted_iota(jnp.int32, (ATT_HEADS, nb), 1)
        lse_ref[...] = jnp.where(lane8 == b, mx + jnp.log(den), lse_ref[...])
        co_ref[t, 0:ATT_W, :] = shift_in(kblk, kcol)
        vblk = c_ref[t, ATT_W:2 * ATT_W, :]
        parts = []
        for h in range(ATT_HEADS):
            sl = slice(h * ATT_HD, (h + 1) * ATT_HD)
            parts.append(jnp.sum(vblk[sl] * p[h:h + 1, :], axis=1, keepdims=True) + p_self[h:h + 1, :] * vcol[sl])
        ocol = jnp.concatenate(parts, axis=0)
        o_ref[...] = jnp.where(lane_b, ocol, o_ref[...])
        co_ref[t, ATT_W:2 * ATT_W, :] = shift_in(vblk, vcol)
        return carry

    lax.fori_loop(0, bt, step, 0)


def _attn_sample(cache_t, q_t, k_t, v_t, gi, bt):
    window, dil = ATT_GROUPS[gi]
    nb = cache_t.shape[0]
    res = lambda shape: pl.BlockSpec(shape, lambda i: (gi, 0, 0))
    return pl.pallas_call(
        functools.partial(_attn_sample_kernel, dil=dil, bt=bt),
        grid=(nb // bt,),
        in_specs=[
            pl.BlockSpec((bt, 2 * ATT_W, window), lambda i: (i, 0, 0)),
            res((None, ATT_W, nb)), res((None, ATT_W, nb)), res((None, ATT_W, nb)),
        ],
        out_specs=[
            pl.BlockSpec((bt, 2 * ATT_W, window), lambda i: (i, 0, 0)),
            pl.BlockSpec((ATT_W, nb), lambda i: (0, 0)),
            pl.BlockSpec((ATT_HEADS, nb), lambda i: (0, 0)),
        ],
        out_shape=[
            jax.ShapeDtypeStruct(cache_t.shape, F32),
            jax.ShapeDtypeStruct((ATT_W, nb), F32),
            jax.ShapeDtypeStruct((ATT_HEADS, nb), F32),
        ],
        compiler_params=_params(("arbitrary",)),
        name=f"attn_sample_w{window}",
    )(cache_t, q_t, k_t, v_t)


def _rope_tables(pos):
    inv = jnp.exp(-math.log(ROPE_THETA) * jnp.arange(ROT_HALF, dtype=F32) * (2.0 / ROT_DIM))
    ang = pos.astype(F32)[:, None] * inv[None, :]
    return jnp.cos(ang), jnp.sin(ang)


def _rope_lane_tables(pos):
    cos, sin = _rope_tables(pos)
    n = pos.shape[0]
    ones = jnp.ones((n, ATT_HD - ROT_DIM), F32)
    zeros = jnp.zeros((n, ATT_HD - ROT_DIM), F32)
    z8 = jnp.zeros((n, ROT_HALF), F32)
    cos_t = jnp.concatenate([cos, cos, ones], axis=1)
    sa_t = jnp.concatenate([-sin, z8, zeros], axis=1)
    sb_t = jnp.concatenate([z8, sin, zeros], axis=1)
    return tuple(jnp.concatenate([a, a], axis=1) for a in (cos_t, sa_t, sb_t))


def kernel(x_prompt, x_sample, state_gdn, state_conv, cache_kv_w128, cache_kv_w512, cache_kv_w2048,
           norm1_w, w_in, conv_w, a_log, dt_bias, gdn_norm_w, q_norm_w, k_norm_w, w_out, norm2_w,
           w_router_group, w_router_expert, w_gate_up, w_down):
    bp, lp, _ = x_prompt.shape
    nb = x_sample.shape[0]
    assert x_sample.shape[1] == 1 and state_gdn.shape[0] == 1
    caches = (cache_kv_w128, cache_kv_w512, cache_kv_w2048)

    w_t = jnp.transpose(w_in[0])
    n_gdn = GDN_BLOCKS * LANES
    w_main = jnp.transpose(jnp.concatenate([w_t[:n_gdn], w_t[n_gdn + 2 * GDN_HEADS:]], axis=0)).astype(BF16)
    w_ba = jnp.pad(jnp.transpose(w_t[n_gdn:n_gdn + 2 * GDN_HEADS]), ((0, 0), (0, LANES - 2 * GDN_HEADS))).astype(BF16)
    conv_wt = jnp.transpose(conv_w[0])
    pad4 = lambda v: jnp.pad(v, (GDN_HEADS, LANES - 2 * GDN_HEADS))[None, :]
    alog_row = pad4(a_log[0])
    dtb_row = pad4(dt_bias[0])
    gnw = gdn_norm_w[0][None, :]
    n1 = norm1_w[0][None, :]
    n2 = norm2_w[0][None, :]
    w_out_b = w_out[0].astype(BF16)
    wgu_b = w_gate_up[0].astype(BF16)
    wd_b = w_down[0].astype(BF16)
    w_router = jnp.pad(jnp.concatenate([w_router_expert[0], w_router_group[0]], axis=1),
                       ((0, 0), (0, LANES - MOE_EXPERTS - MOE_GROUPS))).astype(BF16)
    qw_rows = [jnp.tile(q_norm_w[0, g], 2)[None, :] for g in range(3)]
    kw_rows = [jnp.tile(k_norm_w[0, g], 2)[None, :] for g in range(3)]
    tabs_p = _rope_lane_tables(jnp.arange(lp, dtype=jnp.int32))

    xp = x_prompt.reshape(bp * lp, D_MODEL)
    proj_p, ba_p = _inproj(xp, n1, w_main, w_ba, tm=2048)
    oa_p, sg_p, sc_p = _gdn_prompt(proj_p.reshape(N_MAIN_BLOCKS, bp, lp, LANES), ba_p.reshape(bp, lp, LANES),
                                   conv_wt, alog_row, dtb_row, gnw, bp, lp, nseq=math.gcd(bp, 4))
    oa_p = oa_p.reshape(GDN_HEADS, bp * lp, LANES)
    attn_p = []
    kv_p = []
    for gi, jb in enumerate((4, 4, 2)):
        attn_p.append(_attn_prompt(proj_p, tabs_p, qw_rows[gi], kw_rows[gi], gi, bp, lp, jb))
        window = ATT_GROUPS[gi][0]
        kvt = _kv_tail(proj_p, tabs_p, kw_rows[gi], gi, bp, lp)
        kv_p.append(jnp.transpose(kvt.reshape(2 * ATT_CB, bp, window, LANES), (1, 2, 0, 3))
                    .reshape(1, bp, window, 2, ATT_HEADS, ATT_HD))
    h_p, hn_p, gate_p = _outproj(xp, oa_p, attn_p, w_out_b, n2, w_router, tm=512)
    y_p = _moe(hn_p, gate_p, h_p, wgu_b, wd_b, tm=1024)

    xs = x_sample.reshape(nb, D_MODEL)
    proj_s, ba_s = _inproj(xs, n1, w_main, w_ba, tm=nb)
    conv_state = jnp.transpose(state_conv[0], (1, 0, 2))
    sg_s, oa_s, conv_new = _gdn_sample(proj_s, ba_s, conv_state, conv_wt, alog_row, dtb_row, gnw, state_gdn[0])
    cos_s, sin_s = _rope_tables(jnp.full((1,), PAST_LEN, jnp.int32))
    cos_s = jnp.broadcast_to(jnp.transpose(cos_s), (ROT_HALF, nb))
    sin_s = jnp.broadcast_to(jnp.transpose(sin_s), (ROT_HALF, nb))
    qw_cols = jnp.broadcast_to(jnp.tile(q_norm_w[0], (1, ATT_HEADS))[:, :, None], (3, ATT_W, nb))
    kw_cols = jnp.broadcast_to(jnp.tile(k_norm_w[0], (1, ATT_HEADS))[:, :, None], (3, ATT_W, nb))
    q_t, k_t, v_t = _attn_sample_prep(proj_s, qw_cols, kw_cols, cos_s, sin_s)
    attn_s = []
    kv_s = []
    for gi in range(3):
        window = ATT_GROUPS[gi][0]
        cache_t = jnp.transpose(caches[gi][0], (0, 2, 3, 4, 1)).reshape(nb, 2 * ATT_W, window)
        bt = math.gcd(nb, max(1, SAMPLE_BLOCK_BYTES // (2 * ATT_W * window * 4)))
        new_cache, o_t, lse_t = _attn_sample(cache_t, q_t, k_t, v_t, gi, bt)
        kv_s.append(jnp.transpose(new_cache.reshape(nb, 2, ATT_HEADS, ATT_HD, window), (0, 4, 1, 2, 3))[None])
        o_rows = jnp.transpose(o_t.reshape(ATT_CB, LANES, nb), (0, 2, 1))
        lse_rows = jnp.transpose(jnp.repeat(jnp.transpose(lse_t), ATT_HD, axis=1).reshape(nb, ATT_CB, LANES), (1, 0, 2))
        attn_s.append((o_rows, lse_rows))
    h_s, hn_s, gate_s = _outproj(xs, oa_s, attn_s, w_out_b, n2, w_router, tm=nb)
    y_s = _moe(hn_s, gate_s, h_s, wgu_b, wd_b, tm=nb)

    return (y_p.reshape(bp, lp, D_MODEL), y_s.reshape(nb, 1, D_MODEL),
            sg_p[None], sc_p[None], kv_p[0], kv_p[1], kv_p[2],
            sg_s[None], jnp.transpose(conv_new, (1, 0, 2))[None], kv_s[0], kv_s[1], kv_s[2])
```

```python
import functools
import math

import jax
import jax.numpy as jnp
import numpy as np
from jax import lax
from jax.experimental import pallas as pl
from jax.experimental.pallas import tpu as pltpu

F32 = jnp.float32
BF16 = jnp.bfloat16

LANES = 128
D_MODEL = 1024
GDN_HEADS = 4
GDN_D = 128
GDN_CONV = 4
GDN_CH = 3 * GDN_HEADS * GDN_D
GDN_CB = GDN_CH // LANES
GDN_BLOCKS = 16
ATT_GROUPS = ((128, 1), (512, 4), (2048, 16))
ATT_HEADS = 8
ATT_HD = 64
ATT_W = ATT_HEADS * ATT_HD
ATT_CB = ATT_W // LANES
SPAN = 128
ROT_DIM = 16
ROT_HALF = ROT_DIM // 2
ROPE_THETA = 500000.0
MOE_GROUPS = 4
MOE_PER_GROUP = 8
MOE_EXPERTS = 32
MOE_FF = 256
EPS = 1e-6
PAST_LEN = 8192
NEG = -1e30
N_MAIN_BLOCKS = GDN_BLOCKS + 3 * 3 * ATT_CB
INPROJ_TN = (N_MAIN_BLOCKS // 4) * LANES
VMEM_LIMIT = 48 * 1024 * 1024
MOE_VMEM_LIMIT = 56 * 1024 * 1024
SAMPLE_BLOCK_BYTES = 8 * 1024 * 1024


def _dot(a, b):
    return jnp.dot(a, b, preferred_element_type=F32)


def _dot_nt(a, b):
    return lax.dot_general(a, b, (((1,), (1,)), ((), ())), preferred_element_type=F32)


def _bdot(a, b):
    return _dot(a.astype(BF16), b.astype(BF16))


def _split3(x):
    hi = x.astype(BF16)
    r1 = x - hi.astype(F32)
    mid = r1.astype(BF16)
    lo = (r1 - mid.astype(F32)).astype(BF16)
    return hi, mid, lo


def _dot_sel_left(sel_bf16, x):
    hi, mid, lo = _split3(x)
    return _dot(sel_bf16, hi) + _dot(sel_bf16, mid) + _dot(sel_bf16, lo)


def _dot_sel_right(x, sel_bf16):
    hi, mid, lo = _split3(x)
    return _dot(hi, sel_bf16) + _dot(mid, sel_bf16) + _dot(lo, sel_bf16)


def _silu(x):
    return x * jax.nn.sigmoid(x)


def _softplus(x):
    return jnp.maximum(x, 0.0) + jnp.log1p(jnp.exp(-jnp.abs(x)))


def _params(sem, vmem=VMEM_LIMIT):
    return pltpu.CompilerParams(dimension_semantics=sem, vmem_limit_bytes=vmem)


def _inproj_kernel(x_ref, nw_ref, w_ref, wba_ref, out_ref, ba_ref, xn_ref, *, nblk):
    @pl.when(pl.program_id(1) == 0)
    def _():
        x = x_ref[...]
        ms = jnp.mean(x * x, axis=-1, keepdims=True)
        xn = ((x * lax.rsqrt(ms + EPS)) * nw_ref[...]).astype(BF16)
        xn_ref[...] = xn
        ba_ref[...] = _dot(xn, wba_ref[...])

    res = _dot(xn_ref[...], w_ref[...])
    for jj in range(nblk):
        out_ref[jj] = res[:, jj * LANES:(jj + 1) * LANES]


def _inproj(x2d, norm_w, w_main, w_ba, tm, tn=512):
    t = x2d.shape[0]
    ncol = w_main.shape[1]
    nblk = tn // LANES
    return pl.pallas_call(
        functools.partial(_inproj_kernel, nblk=nblk),
        grid=(t // tm, ncol // tn),
        in_specs=[
            pl.BlockSpec((tm, D_MODEL), lambda i, j: (i, 0)),
            pl.BlockSpec((1, D_MODEL), lambda i, j: (0, 0)),
            pl.BlockSpec((D_MODEL, tn), lambda i, j: (0, j)),
            pl.BlockSpec((D_MODEL, LANES), lambda i, j: (0, 0)),
        ],
        out_specs=[
            pl.BlockSpec((nblk, tm, LANES), lambda i, j: (j, i, 0)),
            pl.BlockSpec((tm, LANES), lambda i, j: (i, 0)),
        ],
        out_shape=[
            jax.ShapeDtypeStruct((ncol // LANES, t, LANES), F32),
            jax.ShapeDtypeStruct((t, LANES), F32),
        ],
        scratch_shapes=[pltpu.VMEM((tm, D_MODEL), BF16)],
        compiler_params=_params(("parallel", "arbitrary")),
        name="inproj",
    )(x2d, norm_w, w_main, w_ba)


def _gates(ba, alog_row, dtb_row):
    beta = jax.nn.sigmoid(ba)
    g = -jnp.exp(alog_row) * _softplus(ba + dtb_row)
    return beta, g


def _l2norm(x):
    return x * lax.rsqrt(jnp.sum(x * x, axis=-1, keepdims=True) + EPS)


def _gdn_out_norm(o, gnw, z):
    on = (o * lax.rsqrt(jnp.mean(o * o, axis=-1, keepdims=True) + EPS)) * gnw
    return on * _silu(z)


def _mm3(a, b):
    ah = a.astype(BF16)
    al = (a - ah.astype(F32)).astype(BF16)
    bh = b.astype(BF16)
    bl = (b - bh.astype(F32)).astype(BF16)
    return _dot(ah, bh) + _dot(ah, bl) + _dot(al, bh)


def _gdn_prompt_kernel(x_ref, ba_ref, cw_ref, alog_ref, dtb_ref, gnw_ref,
                       oa_ref, so_ref, co_ref, s_ref, cbuf_ref, *, nchunks, nseq, refine):
    c = pl.program_id(1)
    ch = SPAN

    @pl.when(c == 0)
    def _():
        s_ref[...] = jnp.zeros_like(s_ref)
        cbuf_ref[:, :, 0:8, :] = jnp.zeros((nseq, GDN_CB, 8, LANES), F32)

    cw = cw_ref[...]
    row = lax.broadcasted_iota(jnp.int32, (ch, ch), 0)
    col = lax.broadcasted_iota(jnp.int32, (ch, ch), 1)
    causal = row >= col
    strict = row > col
    tril = jnp.where(causal, 1.0, 0.0).astype(BF16)
    eye = jnp.where(row == col, 1.0, 0.0)

    @pl.when(c == nchunks - 1)
    def _():
        for sq in range(nseq):
            for cb in range(GDN_CB):
                co_ref[sq, :, cb * LANES:(cb + 1) * LANES] = x_ref[cb, sq, ch - 3:ch, :]

    chains = [(sq, h) for sq in range(nseq) for h in range(GDN_HEADS)]
    qs, ks, kbs, rhss, decays, egs, kdecs, elast = [], [], [], [], [], [], [], []
    for sq in range(nseq):
        qkv = []
        for cb in range(GDN_CB):
            cbuf_ref[sq, cb, 8:8 + ch, :] = x_ref[cb, sq]
            acc = None
            for tap in range(GDN_CONV):
                term = cw[tap:tap + 1, cb * LANES:(cb + 1) * LANES] * cbuf_ref[sq, cb, 5 + tap:5 + tap + ch, :]
                acc = term if acc is None else acc + term
            qkv.append(_silu(acc))
            cbuf_ref[sq, cb, 0:8, :] = cbuf_ref[sq, cb, ch:ch + 8, :]

        beta_t, g_t = _gates(ba_ref[sq], alog_ref[...], dtb_ref[...])
        gc = _dot_sel_left(tril, g_t)
        gct = jnp.transpose(gc)
        for h in range(GDN_HEADS):
            gl = GDN_HEADS + h
            gcol = gc[:, gl:gl + 1]
            grow = gct[gl:gl + 1, :]
            bcol = beta_t[:, h:h + 1]
            glast = gc[ch - 1:ch, gl:gl + 1]
            k = _l2norm(qkv[GDN_HEADS + h])
            kb = k * bcol
            eg = jnp.exp(gcol)
            qs.append(_l2norm(qkv[h]) * (GDN_D ** -0.5))
            ks.append(k.astype(BF16))
            kbs.append(kb.astype(BF16))
            rhss.append(jnp.concatenate([qkv[2 * GDN_HEADS + h] * bcol, kb * eg], axis=1))
            decays.append(jnp.exp(jnp.where(causal, gcol - grow, NEG)))
            egs.append(eg)
            kdecs.append(jnp.transpose(k * jnp.exp(glast - gcol)).astype(BF16))
            elast.append(jnp.exp(glast))

    n = len(chains)
    a_low = [jnp.where(strict, _dot_nt(kbs[i], ks[i]) * decays[i], 0.0) for i in range(n)]
    intra = [jnp.where(causal, _dot_nt(qs[i].astype(BF16), ks[i]) * decays[i], 0.0).astype(BF16) for i in range(n)]
    m = [-a for a in a_low]
    p = [eye + mi for mi in m]
    for _ in range(6):
        m = [_bdot(mi, mi) for mi in m]
        p = [pi + _bdot(pi, mi) for pi, mi in zip(p, m)]
    pb = [pi.astype(BF16) for pi in p]
    x = [_dot(pb[i], rhss[i].astype(BF16)) for i in range(n)]
    for _ in range(refine):
        res = [rhss[i] - x[i] - _mm3(a_low[i], x[i]) for i in range(n)]
        x = [x[i] + _dot(pb[i], res[i].astype(BF16)) for i in range(n)]
    sb = [s_ref[sq, h].astype(BF16) for sq, h in chains]
    v_new = [(x[i][:, :GDN_D] - _dot(x[i][:, GDN_D:].astype(BF16), sb[i])).astype(BF16) for i in range(n)]
    o = [_dot((qs[i] * egs[i]).astype(BF16), sb[i]) + _dot(intra[i], v_new[i]) for i in range(n)]
    for i, (sq, h) in enumerate(chains):
        s_ref[sq, h] = s_ref[sq, h] * elast[i] + _dot(kdecs[i], v_new[i])
        oa_ref[h, sq] = _gdn_out_norm(o[i], gnw_ref[...], x_ref[3 * GDN_HEADS + h, sq])

    @pl.when(c == nchunks - 1)
    def _():
        so_ref[...] = s_ref[...]


def _gdn_prompt(proj, ba, conv_wt, alog_row, dtb_row, gnw, batch, seq, nseq, refine=1):
    nchunks = seq // SPAN
    return pl.pallas_call(
        functools.partial(_gdn_prompt_kernel, nchunks=nchunks, nseq=nseq, refine=refine),
        grid=(batch // nseq, nchunks),
        in_specs=[
            pl.BlockSpec((GDN_BLOCKS, nseq, SPAN, LANES), lambda b, c: (0, b, c, 0)),
            pl.BlockSpec((nseq, SPAN, LANES), lambda b, c: (b, c, 0)),
            pl.BlockSpec((GDN_CONV, GDN_CH), lambda b, c: (0, 0)),
            pl.BlockSpec((1, LANES), lambda b, c: (0, 0)),
            pl.BlockSpec((1, LANES), lambda b, c: (0, 0)),
            pl.BlockSpec((1, LANES), lambda b, c: (0, 0)),
        ],
        out_specs=[
            pl.BlockSpec((GDN_HEADS, nseq, SPAN, LANES), lambda b, c: (0, b, c, 0)),
            pl.BlockSpec((nseq, GDN_HEADS, GDN_D, GDN_D), lambda b, c: (b, 0, 0, 0)),
            pl.BlockSpec((nseq, GDN_CONV - 1, GDN_CH), lambda b, c: (b, 0, 0)),
        ],
        out_shape=[
            jax.ShapeDtypeStruct((GDN_HEADS, batch, seq, LANES), F32),
            jax.ShapeDtypeStruct((batch, GDN_HEADS, GDN_D, GDN_D), F32),
            jax.ShapeDtypeStruct((batch, GDN_CONV - 1, GDN_CH), F32),
        ],
        scratch_shapes=[
            pltpu.VMEM((nseq, GDN_HEADS, GDN_D, GDN_D), F32),
            pltpu.VMEM((nseq, GDN_CB, SPAN + 8, LANES), F32),
        ],
        compiler_params=_params(("parallel", "arbitrary")),
        name="gdn_prompt",
    )(proj, ba, conv_wt, alog_row, dtb_row, gnw)


def _head_norm_rope(x, w, cos, sa, sb, seg_lo):
    sq = x * x
    s0 = jnp.sum(jnp.where(seg_lo, sq, 0.0), axis=1, keepdims=True)
    s1 = jnp.sum(jnp.where(seg_lo, 0.0, sq), axis=1, keepdims=True)
    ms = jnp.where(seg_lo, s0, s1) * (1.0 / ATT_HD)
    xn = (x * lax.rsqrt(ms + EPS)) * w
    return xn * cos + pltpu.roll(xn, LANES - ROT_HALF, 1) * sa + pltpu.roll(xn, ROT_HALF, 1) * sb


def _attn_prompt_kernel(q_ref, k_ref, v_ref, cos_ref, sa_ref, sb_ref, qw_ref, kw_ref,
                        o_ref, lse_ref, kprev_ref, vprev_ref, *, dil, jb, ru):
    n = pl.program_id(2)
    row = lax.broadcasted_iota(jnp.int32, (SPAN, 2 * SPAN), 0)
    col = lax.broadcasted_iota(jnp.int32, (SPAN, 2 * SPAN), 1)
    mask = jnp.logical_or(col <= row, jnp.logical_and(col - SPAN >= row, n > 0))
    lane = lax.broadcasted_iota(jnp.int32, (SPAN, LANES), 1)
    seg_lo = lane < ATT_HD
    lane2 = lax.broadcasted_iota(jnp.int32, (2 * SPAN, LANES), 1)
    in_head2 = [lane2 < ATT_HD, lane2 >= ATT_HD]
    er = lax.broadcasted_iota(jnp.int32, (LANES, LANES), 0)
    ec = lax.broadcasted_iota(jnp.int32, (LANES, LANES), 1)
    seg_sum = jnp.where((er < ATT_HD) == (ec < ATT_HD), 1.0, 0.0).astype(BF16)
    qw = qw_ref[...]
    kw = kw_ref[...]

    @pl.when(n == 0)
    def _():
        kprev_ref[...] = jnp.zeros_like(kprev_ref)
        vprev_ref[...] = jnp.zeros_like(vprev_ref)

    def body(it, carry):
        rs = [it * ru + u for u in range(ru)]
        rows = [pl.ds(r, SPAN, stride=dil) if dil > 1 else pl.ds(0, SPAN) for r in rs]
        blocks = [(u, j) for u in range(ru) for j in range(jb)]
        nblk = len(blocks)
        tabs = [(cos_ref[rows[u], :], sa_ref[rows[u], :], sb_ref[rows[u], :]) for u in range(ru)]
        xs = [q_ref[j, rows[u], :] for u, j in blocks] + [k_ref[j, rows[u], :] for u, j in blocks]
        ws = [qw] * nblk + [kw] * nblk
        sq = [x * x for x in xs]
        sq_hi = [s.astype(BF16) for s in sq]
        sq_lo = [(s - h.astype(F32)).astype(BF16) for s, h in zip(sq, sq_hi)]
        ssq = [_dot(h, seg_sum) + _dot(l, seg_sum) for h, l in zip(sq_hi, sq_lo)]
        xn = [(x * lax.rsqrt(s * (1.0 / ATT_HD) + EPS)) * w for x, s, w in zip(xs, ssq, ws)]
        roped = []
        for i, v in enumerate(xn):
            cos, sa, sb = tabs[blocks[i % nblk][0]]
            roped.append(v * cos + pltpu.roll(v, LANES - ROT_HALF, 1) * sa + pltpu.roll(v, ROT_HALF, 1) * sb)
        qb = [(roped[bi] * (ATT_HD ** -0.5)).astype(BF16) for bi in range(nblk)]
        kcat, vcat = [], []
        for bi, (u, j) in enumerate(blocks):
            sl = slice(j * LANES, (j + 1) * LANES)
            kb = roped[nblk + bi].astype(BF16)
            vb = v_ref[j, rows[u], :].astype(BF16)
            kcat.append(jnp.concatenate([kb, kprev_ref[rs[u], :, sl]], axis=0))
            vcat.append(jnp.concatenate([vb, vprev_ref[rs[u], :, sl]], axis=0))
            kprev_ref[rs[u], :, sl] = kb
            vprev_ref[rs[u], :, sl] = vb
        heads = [(bi, hh) for bi in range(nblk) for hh in range(2)]
        in_head = [seg_lo, jnp.logical_not(seg_lo)]
        qh = [jnp.where(in_head[hh], qb[bi], jnp.zeros_like(qb[bi])) for bi, hh in heads]
        s = [jnp.where(mask, _dot_nt(qh[i], kcat[bi]), NEG) for i, (bi, hh) in enumerate(heads)]
        mx = [jnp.max(a, axis=1, keepdims=True) for a in s]
        p = [jnp.exp(a - m).astype(BF16) for a, m in zip(s, mx)]
        ones = jnp.ones((2 * SPAN, LANES), BF16)
        acc = [_dot(p[i], jnp.where(in_head2[hh], vcat[bi], ones)) for i, (bi, hh) in enumerate(heads)]
        for bi, (u, j) in enumerate(blocks):
            a0, a1 = acc[2 * bi], acc[2 * bi + 1]
            den = pltpu.roll(jnp.where(seg_lo, a1, a0), ATT_HD, 1)
            o_ref[j, rows[u], :] = jnp.where(seg_lo, a0, a1) / den
            lse_ref[j, rows[u], :] = jnp.where(seg_lo, mx[2 * bi], mx[2 * bi + 1]) + jnp.log(den)
        return carry

    lax.fori_loop(0, dil // ru, body, 0)


def _attn_prompt(proj, tabs, qw, kw, gi, batch, seq, jb):
    window, dil = ATT_GROUPS[gi]
    sup = SPAN * dil
    nb = seq // sup
    t = batch * seq
    njg = ATT_CB // jb
    ru = min(dil, ATT_CB // jb)
    base = (GDN_BLOCKS + gi * 3 * ATT_CB) // jb

    def proj_spec(which):
        return pl.BlockSpec((jb, sup, LANES), lambda b, jg, n: (base + which * njg + jg, b * nb + n, 0))

    tab_spec = pl.BlockSpec((sup, LANES), lambda b, jg, n: (n, 0))
    w_spec = pl.BlockSpec((1, LANES), lambda b, jg, n: (0, 0))
    return pl.pallas_call(
        functools.partial(_attn_prompt_kernel, dil=dil, jb=jb, ru=ru),
        grid=(batch, njg, nb),
        in_specs=[proj_spec(0), proj_spec(1), proj_spec(2), tab_spec, tab_spec, tab_spec, w_spec, w_spec],
        out_specs=[
            pl.BlockSpec((jb, sup, LANES), lambda b, jg, n: (jg, b * nb + n, 0)),
            pl.BlockSpec((jb, sup, LANES), lambda b, jg, n: (jg, b * nb + n, 0)),
        ],
        out_shape=[
            jax.ShapeDtypeStruct((ATT_CB, t, LANES), F32),
            jax.ShapeDtypeStruct((ATT_CB, t, LANES), F32),
        ],
        scratch_shapes=[
            pltpu.VMEM((dil, SPAN, jb * LANES), BF16),
            pltpu.VMEM((dil, SPAN, jb * LANES), BF16),
        ],
        compiler_params=_params(("parallel", "parallel", "arbitrary")),
        name=f"attn_prompt_w{window}",
    )(proj, proj, proj, tabs[0], tabs[1], tabs[2], qw, kw)


def _kv_tail_kernel(k_ref, v_ref, cos_ref, sa_ref, sb_ref, kw_ref, out_ref):
    rows = k_ref.shape[1]
    lane = lax.broadcasted_iota(jnp.int32, (rows, LANES), 1)
    seg_lo = lane < ATT_HD
    for j in range(ATT_CB):
        out_ref[j] = _head_norm_rope(k_ref[j], kw_ref[...], cos_ref[...], sa_ref[...], sb_ref[...], seg_lo)
        out_ref[ATT_CB + j] = v_ref[j]


def _kv_tail(proj, tabs, kw, gi, batch, seq):
    window, _ = ATT_GROUPS[gi]
    wt = min(window, 512)
    nt = window // wt
    per_seq = seq // wt
    base = (GDN_BLOCKS + gi * 3 * ATT_CB) // ATT_CB

    def rowblk(b, i):
        return b * per_seq + per_seq - nt + i

    tab_spec = pl.BlockSpec((wt, LANES), lambda b, i: (per_seq - nt + i, 0))
    return pl.pallas_call(
        _kv_tail_kernel,
        grid=(batch, nt),
        in_specs=[
            pl.BlockSpec((ATT_CB, wt, LANES), lambda b, i: (base + 1, rowblk(b, i), 0)),
            pl.BlockSpec((ATT_CB, wt, LANES), lambda b, i: (base + 2, rowblk(b, i), 0)),
            tab_spec, tab_spec, tab_spec,
            pl.BlockSpec((1, LANES), lambda b, i: (0, 0)),
        ],
        out_specs=pl.BlockSpec((2 * ATT_CB, wt, LANES), lambda b, i: (0, b * nt + i, 0)),
        out_shape=jax.ShapeDtypeStruct((2 * ATT_CB, batch * window, LANES), F32),
        compiler_params=_params(("parallel", "parallel")),
        name=f"kv_tail_w{window}",
    )(proj, proj, tabs[0], tabs[1], tabs[2], kw)


def _route(logits):
    lane = lax.broadcasted_iota(jnp.int32, logits.shape, 1).astype(F32)
    big = float(4 * LANES)
    is_g = jnp.logical_and(lane >= MOE_EXPERTS, lane < MOE_EXPERTS + MOE_GROUPS)
    gl = jnp.where(is_g, logits, NEG)
    gmax = jnp.max(gl, axis=1, keepdims=True)
    gidx = jnp.min(jnp.where(gl == gmax, lane, big), axis=1, keepdims=True) - MOE_EXPERTS
    p_group = 1.0 / jnp.sum(jnp.where(is_g, jnp.exp(gl - gmax), 0.0), axis=1, keepdims=True)
    lo = gidx * MOE_PER_GROUP
    in_grp = jnp.logical_and(lane >= lo, lane < lo + MOE_PER_GROUP)
    el = jnp.where(in_grp, logits, NEG)
    e1 = jnp.max(el, axis=1, keepdims=True)
    i1 = jnp.min(jnp.where(el == e1, lane, big), axis=1, keepdims=True)
    el2 = jnp.where(lane == i1, NEG, el)
    e2 = jnp.max(el2, axis=1, keepdims=True)
    i2 = jnp.min(jnp.where(el2 == e2, lane, big), axis=1, keepdims=True)
    t = jnp.exp(e2 - e1)
    w1 = 1.0 / (1.0 + t)
    w2 = t / (1.0 + t)
    return jnp.where(lane == i1, w1 * p_group, jnp.where(lane == i2, w2 * p_group, 0.0))


def _outproj_kernel(x_ref, oa_ref, o0_ref, l0_ref, o1_ref, l1_ref, o2_ref, l2_ref, wout_ref, n2_ref, wr_ref,
                    h_ref, hn_ref, gate_ref):
    obs = []
    for j in range(ATT_CB):
        l0, l1, l2 = l0_ref[j], l1_ref[j], l2_ref[j]
        mx = jnp.maximum(jnp.maximum(l0, l1), l2)
        e0 = jnp.exp(l0 - mx)
        e1 = jnp.exp(l1 - mx)
        e2 = jnp.exp(l2 - mx)
        ob = (e0 * o0_ref[j] + e1 * o1_ref[j] + e2 * o2_ref[j]) / (e0 + e1 + e2)
        obs.append(ob.astype(BF16))
    mix = jnp.concatenate([oa_ref[h].astype(BF16) for h in range(GDN_HEADS)] + obs, axis=1)
    acc = x_ref[...] + _dot(mix, wout_ref[...])
    h_ref[...] = acc
    hn = ((acc * lax.rsqrt(jnp.mean(acc * acc, axis=-1, keepdims=True) + EPS)) * n2_ref[...]).astype(BF16)
    hn_ref[...] = hn
    gate_ref[...] = _route(_dot(hn, wr_ref[...]))


def _outproj(x2d, oa, attn, w_out, norm2_w, w_router, tm):
    t = x2d.shape[0]
    specs = [pl.BlockSpec((tm, D_MODEL), lambda i: (i, 0)),
             pl.BlockSpec((GDN_HEADS, tm, LANES), lambda i: (0, i, 0))]
    args = [x2d, oa]
    for o, lse in attn:
        specs.append(pl.BlockSpec((ATT_CB, tm, LANES), lambda i: (0, i, 0)))
        specs.append(pl.BlockSpec((ATT_CB, tm, LANES), lambda i: (0, i, 0)))
        args += [o, lse]
    specs += [pl.BlockSpec((D_MODEL, D_MODEL), lambda i: (0, 0)),
              pl.BlockSpec((1, D_MODEL), lambda i: (0, 0)),
              pl.BlockSpec((D_MODEL, LANES), lambda i: (0, 0))]
    args += [w_out, norm2_w, w_router]
    return pl.pallas_call(
        _outproj_kernel,
        grid=(t // tm,),
        in_specs=specs,
        out_specs=[pl.BlockSpec((tm, D_MODEL), lambda i: (i, 0)),
                   pl.BlockSpec((tm, D_MODEL), lambda i: (i, 0)),
                   pl.BlockSpec((tm, LANES), lambda i: (i, 0))],
        out_shape=[jax.ShapeDtypeStruct((t, D_MODEL), F32),
                   jax.ShapeDtypeStruct((t, D_MODEL), BF16),
                   jax.ShapeDtypeStruct((t, LANES), F32)],
        compiler_params=_params(("parallel",)),
        name="outproj_router",
    )(*args)


def _moe_kernel(hn_ref, gate_ref, h_ref, wgu_ref, wd_ref, y_ref, acc_ref, *, eb, rsub):
    ei = pl.program_id(1)
    tm = hn_ref.shape[0]

    @pl.when(ei == 0)
    def _():
        acc_ref[...] = jnp.zeros_like(acc_ref)

    units = [(k, r) for k in range(eb) for r in range(tm // rsub)]

    def first_half(k, r):
        rows = slice(r * rsub, (r + 1) * rsub)
        return _dot(hn_ref[rows, :], wgu_ref[k])

    def second_half(k, r, gu):
        rows = slice(r * rsub, (r + 1) * rsub)
        hid = _silu(gu[:, :MOE_FF]) * gu[:, MOE_FF:]
        d = _dot(hid.astype(BF16), wd_ref[k])
        gate = gate_ref[rows, :]
        lane = lax.broadcasted_iota(jnp.int32, gate.shape, 1)
        gcol = jnp.sum(jnp.where(lane == ei * eb + k, gate, 0.0), axis=1, keepdims=True)
        acc_ref[rows, :] += gcol * d

    pending = None
    for k, r in units:
        gu = first_half(k, r)
        if pending is not None:
            second_half(*pending)
        pending = (k, r, gu)
    second_half(*pending)

    @pl.when(ei == pl.num_programs(1) - 1)
    def _():
        y_ref[...] = h_ref[...] + acc_ref[...]


def _moe(hn, gate, h, wgu_b, wd_b, tm, eb=8):
    t = hn.shape[0]
    return pl.pallas_call(
        functools.partial(_moe_kernel, eb=eb, rsub=min(tm, 256)),
        grid=(t // tm, MOE_EXPERTS // eb),
        in_specs=[
            pl.BlockSpec((tm, D_MODEL), lambda i, e: (i, 0)),
            pl.BlockSpec((tm, LANES), lambda i, e: (i, 0)),
            pl.BlockSpec((tm, D_MODEL), lambda i, e: (i, 0)),
            pl.BlockSpec((eb, D_MODEL, 2 * MOE_FF), lambda i, e: (e, 0, 0)),
            pl.BlockSpec((eb, MOE_FF, D_MODEL), lambda i, e: (e, 0, 0)),
        ],
        out_specs=pl.BlockSpec((tm, D_MODEL), lambda i, e: (i, 0)),
        out_shape=jax.ShapeDtypeStruct((t, D_MODEL), F32),
        scratch_shapes=[pltpu.VMEM((tm, D_MODEL), F32)],
        compiler_params=_params(("parallel", "arbitrary"), vmem=MOE_VMEM_LIMIT),
        name="moe",
    )(hn, gate, h, wgu_b, wd_b)


def _gdn_sample_kernel(x_ref, ba_ref, cs_ref, cw_ref, alog_ref, dtb_ref, gnw_ref, s_ref,
                       so_ref, oa_ref, co_ref, qt_ref, kt_ref, v_ref, beta_ref, g_ref, *, bt):
    b = pl.program_id(0)
    nb = x_ref.shape[1]

    @pl.when(b == 0)
    def _():
        cw = cw_ref[...]
        qkv = []
        for cb in range(GDN_CB):
            sl = slice(cb * LANES, (cb + 1) * LANES)
            u = x_ref[cb]
            acc = cw[3:4, sl] * u
            for tap in range(GDN_CONV - 1):
                acc = acc + cw[tap:tap + 1, sl] * cs_ref[tap, :, sl]
            qkv.append(_silu(acc))
            co_ref[0, :, sl] = cs_ref[1, :, sl]
            co_ref[1, :, sl] = cs_ref[2, :, sl]
            co_ref[2, :, sl] = u
        for h in range(GDN_HEADS):
            qt_ref[h] = jnp.transpose(_l2norm(qkv[h]) * (GDN_D ** -0.5))
            kt_ref[h] = jnp.transpose(_l2norm(qkv[GDN_HEADS + h]))
            v_ref[h] = qkv[2 * GDN_HEADS + h]
        beta_t, g_t = _gates(ba_ref[...], alog_ref[...], dtb_ref[...])
        beta_ref[...] = beta_t
        g_ref[...] = g_t

    chains = [(t, h) for t in range(bt) for h in range(GDN_HEADS)]
    lane_nb = lax.broadcasted_iota(jnp.int32, (GDN_D, nb), 1)
    is_b = [lane_nb == b * bt + t for t in range(bt)]
    brow = [beta_ref[pl.ds(b * bt + t, 1), :] for t in range(bt)]
    grow = [g_ref[pl.ds(b * bt + t, 1), :] for t in range(bt)]
    kcol = [jnp.sum(jnp.where(is_b[t], kt_ref[h], 0.0), axis=1, keepdims=True) for t, h in chains]
    qcol = [jnp.sum(jnp.where(is_b[t], qt_ref[h], 0.0), axis=1, keepdims=True) for t, h in chains]
    bet = [brow[t][:, h:h + 1] for t, h in chains]
    eg = [jnp.exp(grow[t][:, GDN_HEADS + h:GDN_HEADS + h + 1]) for t, h in chains]
    n = len(chains)
    ws = [jnp.sum((kcol[i] * (bet[i] * eg[i])) * s_ref[t, h], axis=0, keepdims=True) for i, (t, h) in enumerate(chains)]
    qs = [jnp.sum((qcol[i] * eg[i]) * s_ref[t, h], axis=0, keepdims=True) for i, (t, h) in enumerate(chains)]
    intra = [jnp.sum(qcol[i] * kcol[i], axis=0, keepdims=True) for i in range(n)]
    v_new = [v_ref[h, pl.ds(b * bt + t, 1), :] * bet[i] - ws[i] for i, (t, h) in enumerate(chains)]
    outs = [_gdn_out_norm(qs[i] + intra[i] * v_new[i], gnw_ref[...], x_ref[3 * GDN_HEADS + h, pl.ds(b * bt + t, 1), :])
            for i, (t, h) in enumerate(chains)]
    for i, (t, h) in enumerate(chains):
        so_ref[t, h] = s_ref[t, h] * eg[i] + kcol[i] * v_new[i]
        oa_ref[h, pl.ds(b * bt + t, 1), :] = outs[i]


def _gdn_sample(proj, ba, conv_state, conv_wt, alog_row, dtb_row, gnw, state, bt=8):
    nb = state.shape[0]
    full = lambda shape: pl.BlockSpec(shape, lambda b: (0,) * len(shape))
    return pl.pallas_call(
        functools.partial(_gdn_sample_kernel, bt=bt),
        grid=(nb // bt,),
        in_specs=[
            full((GDN_BLOCKS, nb, LANES)),
            full((nb, LANES)),
            full((GDN_CONV - 1, nb, GDN_CH)),
            full((GDN_CONV, GDN_CH)),
            full((1, LANES)), full((1, LANES)), full((1, LANES)),
            pl.BlockSpec((bt, GDN_HEADS, GDN_D, GDN_D), lambda b: (b, 0, 0, 0)),
        ],
        out_specs=[
            pl.BlockSpec((bt, GDN_HEADS, GDN_D, GDN_D), lambda b: (b, 0, 0, 0)),
            full((GDN_HEADS, nb, LANES)),
            full((GDN_CONV - 1, nb, GDN_CH)),
        ],
        out_shape=[
            jax.ShapeDtypeStruct((nb, GDN_HEADS, GDN_D, GDN_D), F32),
            jax.ShapeDtypeStruct((GDN_HEADS, nb, LANES), F32),
            jax.ShapeDtypeStruct((GDN_CONV - 1, nb, GDN_CH), F32),
        ],
        scratch_shapes=[
            pltpu.VMEM((GDN_HEADS, GDN_D, nb), F32),
            pltpu.VMEM((GDN_HEADS, GDN_D, nb), F32),
            pltpu.VMEM((GDN_HEADS, nb, GDN_D), F32),
            pltpu.VMEM((nb, LANES), F32),
            pltpu.VMEM((nb, LANES), F32),
        ],
        compiler_params=_params(("arbitrary",)),
        name="gdn_sample",
    )(proj, ba, conv_state, conv_wt, alog_row, dtb_row, gnw, state)


def _attn_sample_prep_kernel(xq_ref, xk_ref, xv_ref, qw_ref, kw_ref, cos_ref, sin_ref, q_ref, k_ref, v_ref):
    cos = cos_ref[...]
    sin = sin_ref[...]

    def prep(x_ref, w_ref, scale, out):
        for j in range(ATT_CB):
            xt = jnp.transpose(x_ref[j])
            for hh in range(2):
                xh = xt[hh * ATT_HD:(hh + 1) * ATT_HD]
                ms = jnp.mean(xh * xh, axis=0, keepdims=True)
                base = j * LANES + hh * ATT_HD
                xn = (xh * lax.rsqrt(ms + EPS)) * w_ref[base:base + ATT_HD, :]
                x1 = xn[0:ROT_HALF]
                x2 = xn[ROT_HALF:ROT_DIM]
                out[base:base + ROT_HALF, :] = (x1 * cos - x2 * sin) * scale
                out[base + ROT_HALF:base + ROT_DIM, :] = (x2 * cos + x1 * sin) * scale
                out[base + ROT_DIM:base + ATT_HD, :] = xn[ROT_DIM:] * scale

    prep(xq_ref, qw_ref, ATT_HD ** -0.5, q_ref)
    prep(xk_ref, kw_ref, 1.0, k_ref)
    for j in range(ATT_CB):
        v_ref[j * LANES:(j + 1) * LANES, :] = jnp.transpose(xv_ref[j])


def _attn_sample_prep(proj, qw_cols, kw_cols, cos_s, sin_s):
    nb = proj.shape[1]
    grp = lambda g: (g, 0, 0)
    return pl.pallas_call(
        _attn_sample_prep_kernel,
        grid=(3,),
        in_specs=[
            pl.BlockSpec((ATT_CB, nb, LANES), lambda g: (GDN_BLOCKS // ATT_CB + 3 * g, 0, 0)),
            pl.BlockSpec((ATT_CB, nb, LANES), lambda g: (GDN_BLOCKS // ATT_CB + 3 * g + 1, 0, 0)),
            pl.BlockSpec((ATT_CB, nb, LANES), lambda g: (GDN_BLOCKS // ATT_CB + 3 * g + 2, 0, 0)),
            pl.BlockSpec((None, ATT_W, nb), grp),
            pl.BlockSpec((None, ATT_W, nb), grp),
            pl.BlockSpec((ROT_HALF, nb), lambda g: (0, 0)),
            pl.BlockSpec((ROT_HALF, nb), lambda g: (0, 0)),
        ],
        out_specs=[pl.BlockSpec((None, ATT_W, nb), grp)] * 3,
        out_shape=[jax.ShapeDtypeStruct((3, ATT_W, nb), F32)] * 3,
        compiler_params=_params(("parallel",)),
        name="attn_sample_prep",
    )(proj, proj, proj, qw_cols, kw_cols, cos_s, sin_s)


def _attn_sample_kernel(c_ref, q_ref, k_ref, v_ref, co_ref, o_ref, lse_ref, *, dil, bt):
    i = pl.program_id(0)
    w = c_ref.shape[2]
    nb = q_ref.shape[1]
    lane_w = lax.broadcasted_iota(jnp.int32, (ATT_W, w), 1)
    lane_nb = lax.broadcasted_iota(jnp.int32, (ATT_W, nb), 1)

    def shift_in(blk, new_col):
        return jnp.where(lane_w == w - 1, new_col, pltpu.roll(blk, w - 1, 1))

    @pl.when(i == 0)
    def _():
        o_ref[...] = jnp.zeros_like(o_ref)
        lse_ref[...] = jnp.zeros_like(lse_ref)

    def step(t, carry):
        b = i * bt + t
        lane_b = lane_nb == b
        kblk = c_ref[t, 0:ATT_W, :]
        qcol = jnp.sum(jnp.where(lane_b, q_ref[...], 0.0), axis=1, keepdims=True)
        kcol = jnp.sum(jnp.where(lane_b, k_ref[...], 0.0), axis=1, keepdims=True)
        vcol = jnp.sum(jnp.where(lane_b, v_ref[...], 0.0), axis=1, keepdims=True)
        prod = kblk * qcol
        self_prod = qcol * kcol
        s = jnp.concatenate([jnp.sum(prod[h * ATT_HD:(h + 1) * ATT_HD], axis=0, keepdims=True)
                             for h in range(ATT_HEADS)], axis=0)
        s_self = jnp.concatenate([jnp.sum(self_prod[h * ATT_HD:(h + 1) * ATT_HD], axis=0, keepdims=True)
                                  for h in range(ATT_HEADS)], axis=0)
        pos = lax.broadcasted_iota(jnp.int32, (ATT_HEADS, w), 1)
        s = jnp.where((pos & (dil - 1)) == 0, s, NEG)
        mx = jnp.maximum(jnp.max(s, axis=1, keepdims=True), s_self)
        p = jnp.exp(s - mx)
        p_self = jnp.exp(s_self - mx)
        den = jnp.sum(p, axis=1, keepdims=True) + p_self
        p = p / den
        p_self = p_self / den
        lane8 = lax.broadcasted_iota(jnp.int32, (ATT_HEADS, nb), 1)
        lse_ref[...] = jnp.where(lane8 == b, mx + jnp.log(den), lse_ref[...])
        co_ref[t, 0:ATT_W, :] = shift_in(kblk, kcol)
        vblk = c_ref[t, ATT_W:2 * ATT_W, :]
        parts = []
        for h in range(ATT_HEADS):
            sl = slice(h * ATT_HD, (h + 1) * ATT_HD)
            parts.append(jnp.sum(vblk[sl] * p[h:h + 1, :], axis=1, keepdims=True) + p_self[h:h + 1, :] * vcol[sl])
        ocol = jnp.concatenate(parts, axis=0)
        o_ref[...] = jnp.where(lane_b, ocol, o_ref[...])
        co_ref[t, ATT_W:2 * ATT_W, :] = shift_in(vblk, vcol)
        return carry

    lax.fori_loop(0, bt, step, 0)


def _attn_sample(cache_t, q_t, k_t, v_t, gi, bt):
    window, dil = ATT_GROUPS[gi]
    nb = cache_t.shape[0]
    res = lambda shape: pl.BlockSpec(shape, lambda i: (gi, 0, 0))
    return pl.pallas_call(
        functools.partial(_attn_sample_kernel, dil=dil, bt=bt),
        grid=(nb // bt,),
        in_specs=[
            pl.BlockSpec((bt, 2 * ATT_W, window), lambda i: (i, 0, 0)),
            res((None, ATT_W, nb)), res((None, ATT_W, nb)), res((None, ATT_W, nb)),
        ],
        out_specs=[
            pl.BlockSpec((bt, 2 * ATT_W, window), lambda i: (i, 0, 0)),
            pl.BlockSpec((ATT_W, nb), lambda i: (0, 0)),
            pl.BlockSpec((ATT_HEADS, nb), lambda i: (0, 0)),
        ],
        out_shape=[
            jax.ShapeDtypeStruct(cache_t.shape, F32),
            jax.ShapeDtypeStruct((ATT_W, nb), F32),
            jax.ShapeDtypeStruct((ATT_HEADS, nb), F32),
        ],
        compiler_params=_params(("arbitrary",)),
        name=f"attn_sample_w{window}",
    )(cache_t, q_t, k_t, v_t)


def _rope_tables(pos):
    f32 = np.float32
    inv = np.exp(f32(-math.log(ROPE_THETA)) * np.arange(ROT_HALF, dtype=f32) * f32(2.0 / ROT_DIM)).astype(f32)
    ang = np.asarray(pos, dtype=f32)[:, None] * inv[None, :]
    return np.cos(ang).astype(f32), np.sin(ang).astype(f32)


def _rope_lane_tables(pos):
    cos, sin = _rope_tables(pos)
    n = len(pos)
    ones = np.ones((n, ATT_HD - ROT_DIM), np.float32)
    zeros = np.zeros((n, ATT_HD - ROT_DIM), np.float32)
    z8 = np.zeros((n, ROT_HALF), np.float32)
    cos_t = np.concatenate([cos, cos, ones], axis=1)
    sa_t = np.concatenate([-sin, z8, zeros], axis=1)
    sb_t = np.concatenate([z8, sin, zeros], axis=1)
    return tuple(jnp.asarray(np.concatenate([a, a], axis=1)) for a in (cos_t, sa_t, sb_t))


def kernel(x_prompt, x_sample, state_gdn, state_conv, cache_kv_w128, cache_kv_w512, cache_kv_w2048,
           norm1_w, w_in, conv_w, a_log, dt_bias, gdn_norm_w, q_norm_w, k_norm_w, w_out, norm2_w,
           w_router_group, w_router_expert, w_gate_up, w_down):
    bp, lp, _ = x_prompt.shape
    nb = x_sample.shape[0]
    assert x_sample.shape[1] == 1 and state_gdn.shape[0] == 1
    caches = (cache_kv_w128, cache_kv_w512, cache_kv_w2048)

    w_t = jnp.transpose(w_in[0])
    n_gdn = GDN_BLOCKS * LANES
    w_main = jnp.transpose(jnp.concatenate([w_t[:n_gdn], w_t[n_gdn + 2 * GDN_HEADS:]], axis=0)).astype(BF16)
    w_ba = jnp.pad(jnp.transpose(w_t[n_gdn:n_gdn + 2 * GDN_HEADS]), ((0, 0), (0, LANES - 2 * GDN_HEADS))).astype(BF16)
    conv_wt = jnp.transpose(conv_w[0])
    pad4 = lambda v: jnp.pad(v, (GDN_HEADS, LANES - 2 * GDN_HEADS))[None, :]
    alog_row = pad4(a_log[0])
    dtb_row = pad4(dt_bias[0])
    gnw = gdn_norm_w[0][None, :]
    n1 = norm1_w[0][None, :]
    n2 = norm2_w[0][None, :]
    w_out_b = w_out[0].astype(BF16)
    wgu_b = w_gate_up[0].astype(BF16)
    wd_b = w_down[0].astype(BF16)
    w_router = jnp.pad(jnp.concatenate([w_router_expert[0], w_router_group[0]], axis=1),
                       ((0, 0), (0, LANES - MOE_EXPERTS - MOE_GROUPS))).astype(BF16)
    qw_rows = [jnp.tile(q_norm_w[0, g], 2)[None, :] for g in range(3)]
    kw_rows = [jnp.tile(k_norm_w[0, g], 2)[None, :] for g in range(3)]
    tabs_p = _rope_lane_tables(np.arange(lp))

    xp = x_prompt.reshape(bp * lp, D_MODEL)
    proj_p, ba_p = _inproj(xp, n1, w_main, w_ba, tm=1024, tn=INPROJ_TN)
    oa_p, sg_p, sc_p = _gdn_prompt(proj_p.reshape(N_MAIN_BLOCKS, bp, lp, LANES), ba_p.reshape(bp, lp, LANES),
                                   conv_wt, alog_row, dtb_row, gnw, bp, lp, nseq=math.gcd(bp, 4))
    oa_p = oa_p.reshape(GDN_HEADS, bp * lp, LANES)
    attn_p = []
    kv_p = []
    for gi, jb in enumerate((4, 4, 2)):
        attn_p.append(_attn_prompt(proj_p, tabs_p, qw_rows[gi], kw_rows[gi], gi, bp, lp, jb))
        window = ATT_GROUPS[gi][0]
        kvt = _kv_tail(proj_p, tabs_p, kw_rows[gi], gi, bp, lp)
        kv_p.append(jnp.transpose(kvt.reshape(2 * ATT_CB, bp, window, LANES), (1, 2, 0, 3))
                    .reshape(1, bp, window, 2, ATT_HEADS, ATT_HD))
    h_p, hn_p, gate_p = _outproj(xp, oa_p, attn_p, w_out_b, n2, w_router, tm=512)
    y_p = _moe(hn_p, gate_p, h_p, wgu_b, wd_b, tm=1024)

    xs = x_sample.reshape(nb, D_MODEL)
    proj_s, ba_s = _inproj(xs, n1, w_main, w_ba, tm=nb, tn=INPROJ_TN)
    conv_state = jnp.transpose(state_conv[0], (1, 0, 2))
    sg_s, oa_s, conv_new = _gdn_sample(proj_s, ba_s, conv_state, conv_wt, alog_row, dtb_row, gnw, state_gdn[0])
    cos_s, sin_s = _rope_tables(np.full((1,), PAST_LEN))
    cos_s = jnp.asarray(np.broadcast_to(cos_s.T, (ROT_HALF, nb)))
    sin_s = jnp.asarray(np.broadcast_to(sin_s.T, (ROT_HALF, nb)))
    qw_cols = jnp.broadcast_to(jnp.tile(q_norm_w[0], (1, ATT_HEADS))[:, :, None], (3, ATT_W, nb))
    kw_cols = jnp.broadcast_to(jnp.tile(k_norm_w[0], (1, ATT_HEADS))[:, :, None], (3, ATT_W, nb))
    q_t, k_t, v_t = _attn_sample_prep(proj_s, qw_cols, kw_cols, cos_s, sin_s)
    attn_s = []
    kv_s = []
    for gi in range(3):
        window = ATT_GROUPS[gi][0]
        cache_t = jnp.transpose(caches[gi][0], (0, 2, 3, 4, 1)).reshape(nb, 2 * ATT_W, window)
        bt = math.gcd(nb, max(1, SAMPLE_BLOCK_BYTES // (2 * ATT_W * window * 4)))
        new_cache, o_t, lse_t = _attn_sample(cache_t, q_t, k_t, v_t, gi, bt)
        kv_s.append(jnp.transpose(new_cache.reshape(nb, 2, ATT_HEADS, ATT_HD, window), (0, 4, 1, 2, 3))[None])
        o_rows = jnp.transpose(o_t.reshape(ATT_CB, LANES, nb), (0, 2, 1))
        lse_rows = jnp.transpose(jnp.repeat(jnp.transpose(lse_t), ATT_HD, axis=1).reshape(nb, ATT_CB, LANES), (1, 0, 2))
        attn_s.append((o_rows, lse_rows))
    h_s, hn_s, gate_s = _outproj(xs, oa_s, attn_s, w_out_b, n2, w_router, tm=nb)
    y_s = _moe(hn_s, gate_s, h_s, wgu_b, wd_b, tm=nb)

    return (y_p.reshape(bp, lp, D_MODEL), y_s.reshape(nb, 1, D_MODEL),
            sg_p[None], sc_p[None], kv_p[0], kv_p[1], kv_p[2],
            sg_s[None], jnp.transpose(conv_new, (1, 0, 2))[None], kv_s[0], kv_s[1], kv_s[2])
```

```python
import functools
import math

import jax
import jax.numpy as jnp
import numpy as np
from jax import lax
from jax.experimental import pallas as pl
from jax.experimental.pallas import tpu as pltpu

F32 = jnp.float32
BF16 = jnp.bfloat16

LANES = 128
D_MODEL = 1024
GDN_HEADS = 4
GDN_D = 128
GDN_CONV = 4
GDN_CH = 3 * GDN_HEADS * GDN_D
GDN_CB = GDN_CH // LANES
GDN_BLOCKS = 16
ATT_GROUPS = ((128, 1), (512, 4), (2048, 16))
ATT_HEADS = 8
ATT_HD = 64
ATT_W = ATT_HEADS * ATT_HD
ATT_CB = ATT_W // LANES
SPAN = 128
ROT_DIM = 16
ROT_HALF = ROT_DIM // 2
ROPE_THETA = 500000.0
MOE_GROUPS = 4
MOE_PER_GROUP = 8
MOE_EXPERTS = 32
MOE_FF = 256
EPS = 1e-6
PAST_LEN = 8192
NEG = -1e30
N_MAIN_BLOCKS = GDN_BLOCKS + 3 * 3 * ATT_CB
INPROJ_TN = (N_MAIN_BLOCKS // 4) * LANES
VMEM_LIMIT = 48 * 1024 * 1024
MOE_VMEM_LIMIT = 56 * 1024 * 1024
SAMPLE_BLOCK_BYTES = 8 * 1024 * 1024


def _dot(a, b):
    return jnp.dot(a, b, preferred_element_type=F32)


def _dot_nt(a, b):
    return lax.dot_general(a, b, (((1,), (1,)), ((), ())), preferred_element_type=F32)


def _bdot(a, b):
    return _dot(a.astype(BF16), b.astype(BF16))


def _split3(x):
    hi = x.astype(BF16)
    r1 = x - hi.astype(F32)
    mid = r1.astype(BF16)
    lo = (r1 - mid.astype(F32)).astype(BF16)
    return hi, mid, lo


def _dot_sel_left(sel_bf16, x):
    hi, mid, lo = _split3(x)
    return _dot(sel_bf16, hi) + _dot(sel_bf16, mid) + _dot(sel_bf16, lo)


def _dot_sel_right(x, sel_bf16):
    hi, mid, lo = _split3(x)
    return _dot(hi, sel_bf16) + _dot(mid, sel_bf16) + _dot(lo, sel_bf16)


def _silu(x):
    return x * jax.nn.sigmoid(x)


def _softplus(x):
    return jnp.maximum(x, 0.0) + jnp.log1p(jnp.exp(-jnp.abs(x)))


def _params(sem, vmem=VMEM_LIMIT):
    return pltpu.CompilerParams(dimension_semantics=sem, vmem_limit_bytes=vmem)


def _inproj_kernel(x_ref, nw_ref, w_ref, wba_ref, out_ref, ba_ref, xn_ref, *, nblk):
    @pl.when(pl.program_id(1) == 0)
    def _():
        x = x_ref[...]
        ms = jnp.mean(x * x, axis=-1, keepdims=True)
        xn = ((x * lax.rsqrt(ms + EPS)) * nw_ref[...]).astype(BF16)
        xn_ref[...] = xn
        ba_ref[...] = _dot(xn, wba_ref[...])

    res = _dot(xn_ref[...], w_ref[...])
    for jj in range(nblk):
        out_ref[jj] = res[:, jj * LANES:(jj + 1) * LANES]


def _inproj(x2d, norm_w, w_main, w_ba, tm, tn=512):
    t = x2d.shape[0]
    ncol = w_main.shape[1]
    nblk = tn // LANES
    return pl.pallas_call(
        functools.partial(_inproj_kernel, nblk=nblk),
        grid=(t // tm, ncol // tn),
        in_specs=[
            pl.BlockSpec((tm, D_MODEL), lambda i, j: (i, 0)),
            pl.BlockSpec((1, D_MODEL), lambda i, j: (0, 0)),
            pl.BlockSpec((D_MODEL, tn), lambda i, j: (0, j)),
            pl.BlockSpec((D_MODEL, LANES), lambda i, j: (0, 0)),
        ],
        out_specs=[
            pl.BlockSpec((nblk, tm, LANES), lambda i, j: (j, i, 0)),
            pl.BlockSpec((tm, LANES), lambda i, j: (i, 0)),
        ],
        out_shape=[
            jax.ShapeDtypeStruct((ncol // LANES, t, LANES), F32),
            jax.ShapeDtypeStruct((t, LANES), F32),
        ],
        scratch_shapes=[pltpu.VMEM((tm, D_MODEL), BF16)],
        compiler_params=_params(("parallel", "arbitrary")),
        name="inproj",
    )(x2d, norm_w, w_main, w_ba)


def _gates(ba, alog_row, dtb_row):
    beta = jax.nn.sigmoid(ba)
    g = -jnp.exp(alog_row) * _softplus(ba + dtb_row)
    return beta, g


def _l2norm(x):
    return x * lax.rsqrt(jnp.sum(x * x, axis=-1, keepdims=True) + EPS)


def _gdn_out_norm(o, gnw, z):
    on = (o * lax.rsqrt(jnp.mean(o * o, axis=-1, keepdims=True) + EPS)) * gnw
    return on * _silu(z)


def _mm3(a, b):
    ah = a.astype(BF16)
    al = (a - ah.astype(F32)).astype(BF16)
    bh = b.astype(BF16)
    bl = (b - bh.astype(F32)).astype(BF16)
    return _dot(ah, bh) + _dot(ah, bl) + _dot(al, bh)


def _gdn_prompt_kernel(x_ref, ba_ref, cw_ref, alog_ref, dtb_ref, gnw_ref,
                       oa_ref, so_ref, co_ref, s_ref, cbuf_ref, *, nchunks, nseq, refine):
    c = pl.program_id(1)
    ch = SPAN

    @pl.when(c == 0)
    def _():
        s_ref[...] = jnp.zeros_like(s_ref)
        cbuf_ref[:, :, 0:8, :] = jnp.zeros((nseq, GDN_CB, 8, LANES), F32)

    cw = cw_ref[...]
    row = lax.broadcasted_iota(jnp.int32, (ch, ch), 0)
    col = lax.broadcasted_iota(jnp.int32, (ch, ch), 1)
    causal = row >= col
    strict = row > col
    tril = jnp.where(causal, 1.0, 0.0).astype(BF16)
    eye = jnp.where(row == col, 1.0, 0.0)

    @pl.when(c == nchunks - 1)
    def _():
        for sq in range(nseq):
            for cb in range(GDN_CB):
                co_ref[sq, :, cb * LANES:(cb + 1) * LANES] = x_ref[cb, sq, ch - 3:ch, :]

    chains = [(sq, h) for sq in range(nseq) for h in range(GDN_HEADS)]
    qs, ks, kbs, rhss, decays, egs, kdecs, elast = [], [], [], [], [], [], [], []
    for sq in range(nseq):
        qkv = []
        for cb in range(GDN_CB):
            cbuf_ref[sq, cb, 8:8 + ch, :] = x_ref[cb, sq]
            acc = None
            for tap in range(GDN_CONV):
                term = cw[tap:tap + 1, cb * LANES:(cb + 1) * LANES] * cbuf_ref[sq, cb, 5 + tap:5 + tap + ch, :]
                acc = term if acc is None else acc + term
            qkv.append(_silu(acc))
            cbuf_ref[sq, cb, 0:8, :] = cbuf_ref[sq, cb, ch:ch + 8, :]

        beta_t, g_t = _gates(ba_ref[sq], alog_ref[...], dtb_ref[...])
        gc = _dot_sel_left(tril, g_t)
        gct = jnp.transpose(gc)
        for h in range(GDN_HEADS):
            gl = GDN_HEADS + h
            gcol = gc[:, gl:gl + 1]
            grow = gct[gl:gl + 1, :]
            bcol = beta_t[:, h:h + 1]
            glast = gc[ch - 1:ch, gl:gl + 1]
            k = _l2norm(qkv[GDN_HEADS + h])
            kb = k * bcol
            eg = jnp.exp(gcol)
            qs.append(_l2norm(qkv[h]) * (GDN_D ** -0.5))
            ks.append(k.astype(BF16))
            kbs.append(kb.astype(BF16))
            rhss.append(jnp.concatenate([qkv[2 * GDN_HEADS + h] * bcol, kb * eg], axis=1))
            decays.append(jnp.exp(jnp.where(causal, gcol - grow, NEG)))
            egs.append(eg)
            kdecs.append(jnp.transpose(k * jnp.exp(glast - gcol)).astype(BF16))
            elast.append(jnp.exp(glast))

    n = len(chains)
    a_low = [jnp.where(strict, _dot_nt(kbs[i], ks[i]) * decays[i], 0.0) for i in range(n)]
    intra = [jnp.where(causal, _dot_nt(qs[i].astype(BF16), ks[i]) * decays[i], 0.0).astype(BF16) for i in range(n)]
    m = [-a for a in a_low]
    p = [eye + mi for mi in m]
    for _ in range(6):
        m = [_bdot(mi, mi) for mi in m]
        p = [pi + _bdot(pi, mi) for pi, mi in zip(p, m)]
    pb = [pi.astype(BF16) for pi in p]
    x = [_dot(pb[i], rhss[i].astype(BF16)) for i in range(n)]
    for _ in range(refine):
        res = [rhss[i] - x[i] - _mm3(a_low[i], x[i]) for i in range(n)]
        x = [x[i] + _dot(pb[i], res[i].astype(BF16)) for i in range(n)]
    sb = [s_ref[sq, h].astype(BF16) for sq, h in chains]
    v_new = [(x[i][:, :GDN_D] - _dot(x[i][:, GDN_D:].astype(BF16), sb[i])).astype(BF16) for i in range(n)]
    o = [_dot((qs[i] * egs[i]).astype(BF16), sb[i]) + _dot(intra[i], v_new[i]) for i in range(n)]
    for i, (sq, h) in enumerate(chains):
        s_ref[sq, h] = s_ref[sq, h] * elast[i] + _dot(kdecs[i], v_new[i])
        oa_ref[h, sq] = _gdn_out_norm(o[i], gnw_ref[...], x_ref[3 * GDN_HEADS + h, sq])

    @pl.when(c == nchunks - 1)
    def _():
        so_ref[...] = s_ref[...]


def _gdn_prompt(proj, ba, conv_wt, alog_row, dtb_row, gnw, batch, seq, nseq, refine=1):
    nchunks = seq // SPAN
    return pl.pallas_call(
        functools.partial(_gdn_prompt_kernel, nchunks=nchunks, nseq=nseq, refine=refine),
        grid=(batch // nseq, nchunks),
        in_specs=[
            pl.BlockSpec((GDN_BLOCKS, nseq, SPAN, LANES), lambda b, c: (0, b, c, 0)),
            pl.BlockSpec((nseq, SPAN, LANES), lambda b, c: (b, c, 0)),
            pl.BlockSpec((GDN_CONV, GDN_CH), lambda b, c: (0, 0)),
            pl.BlockSpec((1, LANES), lambda b, c: (0, 0)),
            pl.BlockSpec((1, LANES), lambda b, c: (0, 0)),
            pl.BlockSpec((1, LANES), lambda b, c: (0, 0)),
        ],
        out_specs=[
            pl.BlockSpec((GDN_HEADS, nseq, SPAN, LANES), lambda b, c: (0, b, c, 0)),
            pl.BlockSpec((nseq, GDN_HEADS, GDN_D, GDN_D), lambda b, c: (b, 0, 0, 0)),
            pl.BlockSpec((nseq, GDN_CONV - 1, GDN_CH), lambda b, c: (b, 0, 0)),
        ],
        out_shape=[
            jax.ShapeDtypeStruct((GDN_HEADS, batch, seq, LANES), F32),
            jax.ShapeDtypeStruct((batch, GDN_HEADS, GDN_D, GDN_D), F32),
            jax.ShapeDtypeStruct((batch, GDN_CONV - 1, GDN_CH), F32),
        ],
        scratch_shapes=[
            pltpu.VMEM((nseq, GDN_HEADS, GDN_D, GDN_D), F32),
            pltpu.VMEM((nseq, GDN_CB, SPAN + 8, LANES), F32),
        ],
        compiler_params=_params(("parallel", "arbitrary")),
        name="gdn_prompt",
    )(proj, ba, conv_wt, alog_row, dtb_row, gnw)


def _head_norm_rope(x, w, cos, sa, sb, seg_lo):
    sq = x * x
    s0 = jnp.sum(jnp.where(seg_lo, sq, 0.0), axis=1, keepdims=True)
    s1 = jnp.sum(jnp.where(seg_lo, 0.0, sq), axis=1, keepdims=True)
    ms = jnp.where(seg_lo, s0, s1) * (1.0 / ATT_HD)
    xn = (x * lax.rsqrt(ms + EPS)) * w
    return xn * cos + pltpu.roll(xn, LANES - ROT_HALF, 1) * sa + pltpu.roll(xn, ROT_HALF, 1) * sb


def _attn_prompt_kernel(q_ref, k_ref, v_ref, cos_ref, sa_ref, sb_ref, qw_ref, kw_ref,
                        o_ref, lse_ref, kprev_ref, vprev_ref, *, dil, jb, ru):
    n = pl.program_id(2)
    row = lax.broadcasted_iota(jnp.int32, (SPAN, 2 * SPAN), 0)
    col = lax.broadcasted_iota(jnp.int32, (SPAN, 2 * SPAN), 1)
    mask_inner = jnp.logical_or(col <= row, col - SPAN >= row)
    mask = jnp.logical_or(col <= row, jnp.logical_and(col - SPAN >= row, n > 0))
    lane = lax.broadcasted_iota(jnp.int32, (SPAN, LANES), 1)
    seg_lo = lane < ATT_HD
    lane2 = lax.broadcasted_iota(jnp.int32, (2 * SPAN, LANES), 1)
    in_head2 = [lane2 < ATT_HD, lane2 >= ATT_HD]
    er = lax.broadcasted_iota(jnp.int32, (LANES, LANES), 0)
    ec = lax.broadcasted_iota(jnp.int32, (LANES, LANES), 1)
    seg_sum = jnp.where((er < ATT_HD) == (ec < ATT_HD), 1.0, 0.0).astype(BF16)
    qw = qw_ref[...]
    kw = kw_ref[...]

    @pl.when(n == 0)
    def _():
        kprev_ref[...] = jnp.zeros_like(kprev_ref)
        vprev_ref[...] = jnp.zeros_like(vprev_ref)

    def body(it, carry):
        if dil > 1:
            rs = [it * ru + u for u in range(ru)]
            rows = [pl.ds(r, SPAN, stride=dil) for r in rs]
        else:
            rs = [0] * ru
            rows = [pl.ds(u * SPAN, SPAN) for u in range(ru)]
        blocks = [(u, j) for u in range(ru) for j in range(jb)]
        nblk = len(blocks)
        tabs = [(cos_ref[rows[u], :], sa_ref[rows[u], :], sb_ref[rows[u], :]) for u in range(ru)]
        xs = [q_ref[j, rows[u], :] for u, j in blocks] + [k_ref[j, rows[u], :] for u, j in blocks]
        ws = [qw] * nblk + [kw] * nblk
        sq = [x * x for x in xs]
        sq_hi = [s.astype(BF16) for s in sq]
        sq_lo = [(s - h.astype(F32)).astype(BF16) for s, h in zip(sq, sq_hi)]
        ssq = [_dot(h, seg_sum) + _dot(l, seg_sum) for h, l in zip(sq_hi, sq_lo)]
        xn = [(x * lax.rsqrt(s * (1.0 / ATT_HD) + EPS)) * w for x, s, w in zip(xs, ssq, ws)]
        roped = []
        for i, v in enumerate(xn):
            cos, sa, sb = tabs[blocks[i % nblk][0]]
            roped.append(v * cos + pltpu.roll(v, LANES - ROT_HALF, 1) * sa + pltpu.roll(v, ROT_HALF, 1) * sb)
        qb = [(roped[bi] * (ATT_HD ** -0.5)).astype(BF16) for bi in range(nblk)]
        kbs = [roped[nblk + bi].astype(BF16) for bi in range(nblk)]
        vbs = [v_ref[j, rows[u], :].astype(BF16) for u, j in blocks]
        kcat, vcat, masks = [], [], []
        for bi, (u, j) in enumerate(blocks):
            sl = slice(j * LANES, (j + 1) * LANES)
            if dil == 1 and u > 0:
                kp, vp = kbs[bi - jb], vbs[bi - jb]
                masks.append(mask_inner)
            else:
                kp, vp = kprev_ref[rs[u], :, sl], vprev_ref[rs[u], :, sl]
                masks.append(mask)
            kcat.append(jnp.concatenate([kbs[bi], kp], axis=0))
            vcat.append(jnp.concatenate([vbs[bi], vp], axis=0))
            if dil > 1 or u == ru - 1:
                kprev_ref[rs[u], :, sl] = kbs[bi]
                vprev_ref[rs[u], :, sl] = vbs[bi]
        heads = [(bi, hh) for bi in range(nblk) for hh in range(2)]
        in_head = [seg_lo, jnp.logical_not(seg_lo)]
        qh = [jnp.where(in_head[hh], qb[bi], jnp.zeros_like(qb[bi])) for bi, hh in heads]
        s = [jnp.where(masks[bi], _dot_nt(qh[i], kcat[bi]), NEG) for i, (bi, hh) in enumerate(heads)]
        mx = [jnp.max(a, axis=1, keepdims=True) for a in s]
        p = [jnp.exp(a - m).astype(BF16) for a, m in zip(s, mx)]
        ones = jnp.ones((2 * SPAN, LANES), BF16)
        acc = [_dot(p[i], jnp.where(in_head2[hh], vcat[bi], ones)) for i, (bi, hh) in enumerate(heads)]
        for bi, (u, j) in enumerate(blocks):
            a0, a1 = acc[2 * bi], acc[2 * bi + 1]
            den = pltpu.roll(jnp.where(seg_lo, a1, a0), ATT_HD, 1)
            o_ref[j, rows[u], :] = jnp.where(seg_lo, a0, a1) / den
            lse_ref[j, rows[u], :] = jnp.where(seg_lo, mx[2 * bi], mx[2 * bi + 1]) + jnp.log(den)
        return carry

    lax.fori_loop(0, dil // ru if dil > 1 else 1, body, 0)


def _attn_prompt(proj, tabs, qw, kw, gi, batch, seq, jb):
    window, dil = ATT_GROUPS[gi]
    njg = ATT_CB // jb
    ru = 2 * ATT_CB // jb
    sup = SPAN * (dil if dil > 1 else ru)
    assert dil == 1 or dil % ru == 0
    nb = seq // sup
    t = batch * seq
    base = (GDN_BLOCKS + gi * 3 * ATT_CB) // jb

    def proj_spec(which):
        return pl.BlockSpec((jb, sup, LANES), lambda b, jg, n: (base + which * njg + jg, b * nb + n, 0))

    tab_spec = pl.BlockSpec((sup, LANES), lambda b, jg, n: (n, 0))
    w_spec = pl.BlockSpec((1, LANES), lambda b, jg, n: (0, 0))
    return pl.pallas_call(
        functools.partial(_attn_prompt_kernel, dil=dil, jb=jb, ru=ru),
        grid=(batch, njg, nb),
        in_specs=[proj_spec(0), proj_spec(1), proj_spec(2), tab_spec, tab_spec, tab_spec, w_spec, w_spec],
        out_specs=[
            pl.BlockSpec((jb, sup, LANES), lambda b, jg, n: (jg, b * nb + n, 0)),
            pl.BlockSpec((jb, sup, LANES), lambda b, jg, n: (jg, b * nb + n, 0)),
        ],
        out_shape=[
            jax.ShapeDtypeStruct((ATT_CB, t, LANES), F32),
            jax.ShapeDtypeStruct((ATT_CB, t, LANES), F32),
        ],
        scratch_shapes=[
            pltpu.VMEM((dil, SPAN, jb * LANES), BF16),
            pltpu.VMEM((dil, SPAN, jb * LANES), BF16),
        ],
        compiler_params=_params(("parallel", "parallel", "arbitrary")),
        name=f"attn_prompt_w{window}",
    )(proj, proj, proj, tabs[0], tabs[1], tabs[2], qw, kw)


def _kv_tail_kernel(k_ref, v_ref, cos_ref, sa_ref, sb_ref, kw_ref, out_ref):
    rows = k_ref.shape[1]
    lane = lax.broadcasted_iota(jnp.int32, (rows, LANES), 1)
    seg_lo = lane < ATT_HD
    for j in range(ATT_CB):
        out_ref[j] = _head_norm_rope(k_ref[j], kw_ref[...], cos_ref[...], sa_ref[...], sb_ref[...], seg_lo)
        out_ref[ATT_CB + j] = v_ref[j]


def _kv_tail(proj, tabs, kw, gi, batch, seq):
    window, _ = ATT_GROUPS[gi]
    wt = min(window, 512)
    nt = window // wt
    per_seq = seq // wt
    base = (GDN_BLOCKS + gi * 3 * ATT_CB) // ATT_CB

    def rowblk(b, i):
        return b * per_seq + per_seq - nt + i

    tab_spec = pl.BlockSpec((wt, LANES), lambda b, i: (per_seq - nt + i, 0))
    return pl.pallas_call(
        _kv_tail_kernel,
        grid=(batch, nt),
        in_specs=[
            pl.BlockSpec((ATT_CB, wt, LANES), lambda b, i: (base + 1, rowblk(b, i), 0)),
            pl.BlockSpec((ATT_CB, wt, LANES), lambda b, i: (base + 2, rowblk(b, i), 0)),
            tab_spec, tab_spec, tab_spec,
            pl.BlockSpec((1, LANES), lambda b, i: (0, 0)),
        ],
        out_specs=pl.BlockSpec((2 * ATT_CB, wt, LANES), lambda b, i: (0, b * nt + i, 0)),
        out_shape=jax.ShapeDtypeStruct((2 * ATT_CB, batch * window, LANES), F32),
        compiler_params=_params(("parallel", "parallel")),
        name=f"kv_tail_w{window}",
    )(proj, proj, tabs[0], tabs[1], tabs[2], kw)


def _route(logits):
    lane = lax.broadcasted_iota(jnp.int32, logits.shape, 1).astype(F32)
    big = float(4 * LANES)
    is_g = jnp.logical_and(lane >= MOE_EXPERTS, lane < MOE_EXPERTS + MOE_GROUPS)
    gl = jnp.where(is_g, logits, NEG)
    gmax = jnp.max(gl, axis=1, keepdims=True)
    gidx = jnp.min(jnp.where(gl == gmax, lane, big), axis=1, keepdims=True) - MOE_EXPERTS
    p_group = 1.0 / jnp.sum(jnp.where(is_g, jnp.exp(gl - gmax), 0.0), axis=1, keepdims=True)
    lo = gidx * MOE_PER_GROUP
    in_grp = jnp.logical_and(lane >= lo, lane < lo + MOE_PER_GROUP)
    el = jnp.where(in_grp, logits, NEG)
    e1 = jnp.max(el, axis=1, keepdims=True)
    i1 = jnp.min(jnp.where(el == e1, lane, big), axis=1, keepdims=True)
    el2 = jnp.where(lane == i1, NEG, el)
    e2 = jnp.max(el2, axis=1, keepdims=True)
    i2 = jnp.min(jnp.where(el2 == e2, lane, big), axis=1, keepdims=True)
    t = jnp.exp(e2 - e1)
    w1 = 1.0 / (1.0 + t)
    w2 = t / (1.0 + t)
    return jnp.where(lane == i1, w1 * p_group, jnp.where(lane == i2, w2 * p_group, 0.0))


def _outproj_kernel(x_ref, oa_ref, o0_ref, l0_ref, o1_ref, l1_ref, o2_ref, l2_ref, wout_ref, n2_ref, wr_ref,
                    h_ref, hn_ref, gate_ref):
    obs = []
    for j in range(ATT_CB):
        l0, l1, l2 = l0_ref[j], l1_ref[j], l2_ref[j]
        mx = jnp.maximum(jnp.maximum(l0, l1), l2)
        e0 = jnp.exp(l0 - mx)
        e1 = jnp.exp(l1 - mx)
        e2 = jnp.exp(l2 - mx)
        ob = (e0 * o0_ref[j] + e1 * o1_ref[j] + e2 * o2_ref[j]) / (e0 + e1 + e2)
        obs.append(ob.astype(BF16))
    mix = jnp.concatenate([oa_ref[h].astype(BF16) for h in range(GDN_HEADS)] + obs, axis=1)
    acc = x_ref[...] + _dot(mix, wout_ref[...])
    h_ref[...] = acc
    hn = ((acc * lax.rsqrt(jnp.mean(acc * acc, axis=-1, keepdims=True) + EPS)) * n2_ref[...]).astype(BF16)
    hn_ref[...] = hn
    gate_ref[...] = _route(_dot(hn, wr_ref[...]))


def _outproj(x2d, oa, attn, w_out, norm2_w, w_router, tm):
    t = x2d.shape[0]
    specs = [pl.BlockSpec((tm, D_MODEL), lambda i: (i, 0)),
             pl.BlockSpec((GDN_HEADS, tm, LANES), lambda i: (0, i, 0))]
    args = [x2d, oa]
    for o, lse in attn:
        specs.append(pl.BlockSpec((ATT_CB, tm, LANES), lambda i: (0, i, 0)))
        specs.append(pl.BlockSpec((ATT_CB, tm, LANES), lambda i: (0, i, 0)))
        args += [o, lse]
    specs += [pl.BlockSpec((D_MODEL, D_MODEL), lambda i: (0, 0)),
              pl.BlockSpec((1, D_MODEL), lambda i: (0, 0)),
              pl.BlockSpec((D_MODEL, LANES), lambda i: (0, 0))]
    args += [w_out, norm2_w, w_router]
    return pl.pallas_call(
        _outproj_kernel,
        grid=(t // tm,),
        in_specs=specs,
        out_specs=[pl.BlockSpec((tm, D_MODEL), lambda i: (i, 0)),
                   pl.BlockSpec((tm, D_MODEL), lambda i: (i, 0)),
                   pl.BlockSpec((tm, LANES), lambda i: (i, 0))],
        out_shape=[jax.ShapeDtypeStruct((t, D_MODEL), F32),
                   jax.ShapeDtypeStruct((t, D_MODEL), BF16),
                   jax.ShapeDtypeStruct((t, LANES), F32)],
        compiler_params=_params(("parallel",)),
        name="outproj_router",
    )(*args)


def _moe_kernel(hn_ref, gate_ref, h_ref, wgu_ref, wd_ref, y_ref, acc_ref, *, eb, rsub):
    ei = pl.program_id(1)
    tm = hn_ref.shape[0]

    @pl.when(ei == 0)
    def _():
        acc_ref[...] = jnp.zeros_like(acc_ref)

    units = [(k, r) for k in range(eb) for r in range(tm // rsub)]

    def first_half(k, r):
        rows = slice(r * rsub, (r + 1) * rsub)
        return _dot(hn_ref[rows, :], wgu_ref[k])

    def second_half(k, r, gu):
        rows = slice(r * rsub, (r + 1) * rsub)
        hid = _silu(gu[:, :MOE_FF]) * gu[:, MOE_FF:]
        d = _dot(hid.astype(BF16), wd_ref[k])
        gate = gate_ref[rows, :]
        lane = lax.broadcasted_iota(jnp.int32, gate.shape, 1)
        gcol = jnp.sum(jnp.where(lane == ei * eb + k, gate, 0.0), axis=1, keepdims=True)
        acc_ref[rows, :] += gcol * d

    pending = None
    for k, r in units:
        gu = first_half(k, r)
        if pending is not None:
            second_half(*pending)
        pending = (k, r, gu)
    second_half(*pending)

    @pl.when(ei == pl.num_programs(1) - 1)
    def _():
        y_ref[...] = h_ref[...] + acc_ref[...]


def _moe(hn, gate, h, wgu_b, wd_b, tm, eb=8):
    t = hn.shape[0]
    return pl.pallas_call(
        functools.partial(_moe_kernel, eb=eb, rsub=min(tm, 256)),
        grid=(t // tm, MOE_EXPERTS // eb),
        in_specs=[
            pl.BlockSpec((tm, D_MODEL), lambda i, e: (i, 0)),
            pl.BlockSpec((tm, LANES), lambda i, e: (i, 0)),
            pl.BlockSpec((tm, D_MODEL), lambda i, e: (i, 0)),
            pl.BlockSpec((eb, D_MODEL, 2 * MOE_FF), lambda i, e: (e, 0, 0)),
            pl.BlockSpec((eb, MOE_FF, D_MODEL), lambda i, e: (e, 0, 0)),
        ],
        out_specs=pl.BlockSpec((tm, D_MODEL), lambda i, e: (i, 0)),
        out_shape=jax.ShapeDtypeStruct((t, D_MODEL), F32),
        scratch_shapes=[pltpu.VMEM((tm, D_MODEL), F32)],
        compiler_params=_params(("parallel", "arbitrary"), vmem=MOE_VMEM_LIMIT),
        name="moe",
    )(hn, gate, h, wgu_b, wd_b)


def _gdn_sample_kernel(x_ref, ba_ref, cs_ref, cw_ref, alog_ref, dtb_ref, gnw_ref, s_ref,
                       so_ref, oa_ref, co_ref, qt_ref, kt_ref, v_ref, beta_ref, g_ref, *, bt):
    b = pl.program_id(0)
    nb = x_ref.shape[1]

    @pl.when(b == 0)
    def _():
        cw = cw_ref[...]
        qkv = []
        for cb in range(GDN_CB):
            sl = slice(cb * LANES, (cb + 1) * LANES)
            u = x_ref[cb]
            acc = cw[3:4, sl] * u
            for tap in range(GDN_CONV - 1):
                acc = acc + cw[tap:tap + 1, sl] * cs_ref[tap, :, sl]
            qkv.append(_silu(acc))
            co_ref[0, :, sl] = cs_ref[1, :, sl]
            co_ref[1, :, sl] = cs_ref[2, :, sl]
            co_ref[2, :, sl] = u
        for h in range(GDN_HEADS):
            qt_ref[h] = jnp.transpose(_l2norm(qkv[h]) * (GDN_D ** -0.5))
            kt_ref[h] = jnp.transpose(_l2norm(qkv[GDN_HEADS + h]))
            v_ref[h] = qkv[2 * GDN_HEADS + h]
        beta_t, g_t = _gates(ba_ref[...], alog_ref[...], dtb_ref[...])
        beta_ref[...] = beta_t
        g_ref[...] = g_t

    chains = [(t, h) for t in range(bt) for h in range(GDN_HEADS)]
    lane_nb = lax.broadcasted_iota(jnp.int32, (GDN_D, nb), 1)
    is_b = [lane_nb == b * bt + t for t in range(bt)]
    brow = [beta_ref[pl.ds(b * bt + t, 1), :] for t in range(bt)]
    grow = [g_ref[pl.ds(b * bt + t, 1), :] for t in range(bt)]
    kcol = [jnp.sum(jnp.where(is_b[t], kt_ref[h], 0.0), axis=1, keepdims=True) for t, h in chains]
    qcol = [jnp.sum(jnp.where(is_b[t], qt_ref[h], 0.0), axis=1, keepdims=True) for t, h in chains]
    bet = [brow[t][:, h:h + 1] for t, h in chains]
    eg = [jnp.exp(grow[t][:, GDN_HEADS + h:GDN_HEADS + h + 1]) for t, h in chains]
    n = len(chains)
    ws = [jnp.sum((kcol[i] * (bet[i] * eg[i])) * s_ref[t, h], axis=0, keepdims=True) for i, (t, h) in enumerate(chains)]
    qs = [jnp.sum((qcol[i] * eg[i]) * s_ref[t, h], axis=0, keepdims=True) for i, (t, h) in enumerate(chains)]
    intra = [jnp.sum(qcol[i] * kcol[i], axis=0, keepdims=True) for i in range(n)]
    v_new = [v_ref[h, pl.ds(b * bt + t, 1), :] * bet[i] - ws[i] for i, (t, h) in enumerate(chains)]
    outs = [_gdn_out_norm(qs[i] + intra[i] * v_new[i], gnw_ref[...], x_ref[3 * GDN_HEADS + h, pl.ds(b * bt + t, 1), :])
            for i, (t, h) in enumerate(chains)]
    for i, (t, h) in enumerate(chains):
        so_ref[t, h] = s_ref[t, h] * eg[i] + kcol[i] * v_new[i]
        oa_ref[h, pl.ds(b * bt + t, 1), :] = outs[i]


def _gdn_sample(proj, ba, conv_state, conv_wt, alog_row, dtb_row, gnw, state, bt=8):
    nb = state.shape[0]
    full = lambda shape: pl.BlockSpec(shape, lambda b: (0,) * len(shape))
    return pl.pallas_call(
        functools.partial(_gdn_sample_kernel, bt=bt),
        grid=(nb // bt,),
        in_specs=[
            full((GDN_BLOCKS, nb, LANES)),
            full((nb, LANES)),
            full((GDN_CONV - 1, nb, GDN_CH)),
            full((GDN_CONV, GDN_CH)),
            full((1, LANES)), full((1, LANES)), full((1, LANES)),
            pl.BlockSpec((bt, GDN_HEADS, GDN_D, GDN_D), lambda b: (b, 0, 0, 0)),
        ],
        out_specs=[
            pl.BlockSpec((bt, GDN_HEADS, GDN_D, GDN_D), lambda b: (b, 0, 0, 0)),
            full((GDN_HEADS, nb, LANES)),
            full((GDN_CONV - 1, nb, GDN_CH)),
        ],
        out_shape=[
            jax.ShapeDtypeStruct((nb, GDN_HEADS, GDN_D, GDN_D), F32),
            jax.ShapeDtypeStruct((GDN_HEADS, nb, LANES), F32),
            jax.ShapeDtypeStruct((GDN_CONV - 1, nb, GDN_CH), F32),
        ],
        scratch_shapes=[
            pltpu.VMEM((GDN_HEADS, GDN_D, nb), F32),
            pltpu.VMEM((GDN_HEADS, GDN_D, nb), F32),
            pltpu.VMEM((GDN_HEADS, nb, GDN_D), F32),
            pltpu.VMEM((nb, LANES), F32),
            pltpu.VMEM((nb, LANES), F32),
        ],
        compiler_params=_params(("arbitrary",)),
        name="gdn_sample",
    )(proj, ba, conv_state, conv_wt, alog_row, dtb_row, gnw, state)


def _attn_sample_prep_kernel(xq_ref, xk_ref, xv_ref, qw_ref, kw_ref, cos_ref, sin_ref, q_ref, k_ref, v_ref):
    cos = cos_ref[...]
    sin = sin_ref[...]

    def prep(x_ref, w_ref, scale, out):
        for j in range(ATT_CB):
            xt = jnp.transpose(x_ref[j])
            for hh in range(2):
                xh = xt[hh * ATT_HD:(hh + 1) * ATT_HD]
                ms = jnp.mean(xh * xh, axis=0, keepdims=True)
                base = j * LANES + hh * ATT_HD
                xn = (xh * lax.rsqrt(ms + EPS)) * w_ref[base:base + ATT_HD, :]
                x1 = xn[0:ROT_HALF]
                x2 = xn[ROT_HALF:ROT_DIM]
                out[base:base + ROT_HALF, :] = (x1 * cos - x2 * sin) * scale
                out[base + ROT_HALF:base + ROT_DIM, :] = (x2 * cos + x1 * sin) * scale
                out[base + ROT_DIM:base + ATT_HD, :] = xn[ROT_DIM:] * scale

    prep(xq_ref, qw_ref, ATT_HD ** -0.5, q_ref)
    prep(xk_ref, kw_ref, 1.0, k_ref)
    for j in range(ATT_CB):
        v_ref[j * LANES:(j + 1) * LANES, :] = jnp.transpose(xv_ref[j])


def _attn_sample_prep(proj, qw_cols, kw_cols, cos_s, sin_s):
    nb = proj.shape[1]
    grp = lambda g: (g, 0, 0)
    return pl.pallas_call(
        _attn_sample_prep_kernel,
        grid=(3,),
        in_specs=[
            pl.BlockSpec((ATT_CB, nb, LANES), lambda g: (GDN_BLOCKS // ATT_CB + 3 * g, 0, 0)),
            pl.BlockSpec((ATT_CB, nb, LANES), lambda g: (GDN_BLOCKS // ATT_CB + 3 * g + 1, 0, 0)),
            pl.BlockSpec((ATT_CB, nb, LANES), lambda g: (GDN_BLOCKS // ATT_CB + 3 * g + 2, 0, 0)),
            pl.BlockSpec((None, ATT_W, nb), grp),
            pl.BlockSpec((None, ATT_W, nb), grp),
            pl.BlockSpec((ROT_HALF, nb), lambda g: (0, 0)),
            pl.BlockSpec((ROT_HALF, nb), lambda g: (0, 0)),
        ],
        out_specs=[pl.BlockSpec((None, ATT_W, nb), grp)] * 3,
        out_shape=[jax.ShapeDtypeStruct((3, ATT_W, nb), F32)] * 3,
        compiler_params=_params(("parallel",)),
        name="attn_sample_prep",
    )(proj, proj, proj, qw_cols, kw_cols, cos_s, sin_s)


def _attn_sample_kernel(c_ref, q_ref, k_ref, v_ref, co_ref, o_ref, lse_ref, *, dil, bt):
    i = pl.program_id(0)
    w = c_ref.shape[2]
    nb = q_ref.shape[1]
    lane_w = lax.broadcasted_iota(jnp.int32, (ATT_W, w), 1)
    lane_nb = lax.broadcasted_iota(jnp.int32, (ATT_W, nb), 1)

    def shift_in(blk, new_col):
        return jnp.where(lane_w == w - 1, new_col, pltpu.roll(blk, w - 1, 1))

    @pl.when(i == 0)
    def _():
        o_ref[...] = jnp.zeros_like(o_ref)
        lse_ref[...] = jnp.zeros_like(lse_ref)

    def step(t, carry):
        b = i * bt + t
        lane_b = lane_nb == b
        kblk = c_ref[t, 0:ATT_W, :]
        qcol = jnp.sum(jnp.where(lane_b, q_ref[...], 0.0), axis=1, keepdims=True)
        kcol = jnp.sum(jnp.where(lane_b, k_ref[...], 0.0), axis=1, keepdims=True)
        vcol = jnp.sum(jnp.where(lane_b, v_ref[...], 0.0), axis=1, keepdims=True)
        prod = kblk * qcol
        self_prod = qcol * kcol
        s = jnp.concatenate([jnp.sum(prod[h * ATT_HD:(h + 1) * ATT_HD], axis=0, keepdims=True)
                             for h in range(ATT_HEADS)], axis=0)
        s_self = jnp.concatenate([jnp.sum(self_prod[h * ATT_HD:(h + 1) * ATT_HD], axis=0, keepdims=True)
                                  for h in range(ATT_HEADS)], axis=0)
        pos = lax.broadcasted_iota(jnp.int32, (ATT_HEADS, w), 1)
        s = jnp.where((pos & (dil - 1)) == 0, s, NEG)
        mx = jnp.maximum(jnp.max(s, axis=1, keepdims=True), s_self)
        p = jnp.exp(s - mx)
        p_self = jnp.exp(s_self - mx)
        den = jnp.sum(p, axis=1, keepdims=True) + p_self
        p = p / den
        p_self = p_self / den
        lane8 = lax.broadcasted_iota(jnp.int32, (ATT_HEADS, nb), 1)
        lse_ref[...] = jnp.where(lane8 == b, mx + jnp.log(den), lse_ref[...])
        co_ref[t, 0:ATT_W, :] = shift_in(kblk, kcol)
        vblk = c_ref[t, ATT_W:2 * ATT_W, :]
        parts = []
        for h in range(ATT_HEADS):
            sl = slice(h * ATT_HD, (h + 1) * ATT_HD)
            parts.append(jnp.sum(vblk[sl] * p[h:h + 1, :], axis=1, keepdims=True) + p_self[h:h + 1, :] * vcol[sl])
        ocol = jnp.concatenate(parts, axis=0)
        o_ref[...] = jnp.where(lane_b, ocol, o_ref[...])
        co_ref[t, ATT_W:2 * ATT_W, :] = shift_in(vblk, vcol)
        return carry

    lax.fori_loop(0, bt, step, 0)


def _attn_sample(cache_t, q_t, k_t, v_t, gi, bt):
    window, dil = ATT_GROUPS[gi]
    nb = cache_t.shape[0]
    res = lambda shape: pl.BlockSpec(shape, lambda i: (gi, 0, 0))
    return pl.pallas_call(
        functools.partial(_attn_sample_kernel, dil=dil, bt=bt),
        grid=(nb // bt,),
        in_specs=[
            pl.BlockSpec((bt, 2 * ATT_W, window), lambda i: (i, 0, 0)),
            res((None, ATT_W, nb)), res((None, ATT_W, nb)), res((None, ATT_W, nb)),
        ],
        out_specs=[
            pl.BlockSpec((bt, 2 * ATT_W, window), lambda i: (i, 0, 0)),
            pl.BlockSpec((ATT_W, nb), lambda i: (0, 0)),
            pl.BlockSpec((ATT_HEADS, nb), lambda i: (0, 0)),
        ],
        out_shape=[
            jax.ShapeDtypeStruct(cache_t.shape, F32),
            jax.ShapeDtypeStruct((ATT_W, nb), F32),
            jax.ShapeDtypeStruct((ATT_HEADS, nb), F32),
        ],
        compiler_params=_params(("arbitrary",)),
        name=f"attn_sample_w{window}",
    )(cache_t, q_t, k_t, v_t)


def _rope_tables(pos):
    f32 = np.float32
    inv = np.exp(f32(-math.log(ROPE_THETA)) * np.arange(ROT_HALF, dtype=f32) * f32(2.0 / ROT_DIM)).astype(f32)
    ang = np.asarray(pos, dtype=f32)[:, None] * inv[None, :]
    return np.cos(ang).astype(f32), np.sin(ang).astype(f32)


def _rope_lane_tables(pos):
    cos, sin = _rope_tables(pos)
    n = len(pos)
    ones = np.ones((n, ATT_HD - ROT_DIM), np.float32)
    zeros = np.zeros((n, ATT_HD - ROT_DIM), np.float32)
    z8 = np.zeros((n, ROT_HALF), np.float32)
    cos_t = np.concatenate([cos, cos, ones], axis=1)
    sa_t = np.concatenate([-sin, z8, zeros], axis=1)
    sb_t = np.concatenate([z8, sin, zeros], axis=1)
    return tuple(jnp.asarray(np.concatenate([a, a], axis=1)) for a in (cos_t, sa_t, sb_t))


def kernel(x_prompt, x_sample, state_gdn, state_conv, cache_kv_w128, cache_kv_w512, cache_kv_w2048,
           norm1_w, w_in, conv_w, a_log, dt_bias, gdn_norm_w, q_norm_w, k_norm_w, w_out, norm2_w,
           w_router_group, w_router_expert, w_gate_up, w_down):
    bp, lp, _ = x_prompt.shape
    nb = x_sample.shape[0]
    assert x_sample.shape[1] == 1 and state_gdn.shape[0] == 1
    caches = (cache_kv_w128, cache_kv_w512, cache_kv_w2048)

    w_t = jnp.transpose(w_in[0])
    n_gdn = GDN_BLOCKS * LANES
    w_main = jnp.transpose(jnp.concatenate([w_t[:n_gdn], w_t[n_gdn + 2 * GDN_HEADS:]], axis=0)).astype(BF16)
    w_ba = jnp.pad(jnp.transpose(w_t[n_gdn:n_gdn + 2 * GDN_HEADS]), ((0, 0), (0, LANES - 2 * GDN_HEADS))).astype(BF16)
    conv_wt = jnp.transpose(conv_w[0])
    pad4 = lambda v: jnp.pad(v, (GDN_HEADS, LANES - 2 * GDN_HEADS))[None, :]
    alog_row = pad4(a_log[0])
    dtb_row = pad4(dt_bias[0])
    gnw = gdn_norm_w[0][None, :]
    n1 = norm1_w[0][None, :]
    n2 = norm2_w[0][None, :]
    w_out_b = w_out[0].astype(BF16)
    wgu_b = w_gate_up[0].astype(BF16)
    wd_b = w_down[0].astype(BF16)
    w_router = jnp.pad(jnp.concatenate([w_router_expert[0], w_router_group[0]], axis=1),
                       ((0, 0), (0, LANES - MOE_EXPERTS - MOE_GROUPS))).astype(BF16)
    qw_rows = [jnp.tile(q_norm_w[0, g], 2)[None, :] for g in range(3)]
    kw_rows = [jnp.tile(k_norm_w[0, g], 2)[None, :] for g in range(3)]
    tabs_p = _rope_lane_tables(np.arange(lp))

    xp = x_prompt.reshape(bp * lp, D_MODEL)
    proj_p, ba_p = _inproj(xp, n1, w_main, w_ba, tm=1024, tn=INPROJ_TN)
    oa_p, sg_p, sc_p = _gdn_prompt(proj_p.reshape(N_MAIN_BLOCKS, bp, lp, LANES), ba_p.reshape(bp, lp, LANES),
                                   conv_wt, alog_row, dtb_row, gnw, bp, lp, nseq=math.gcd(bp, 4))
    oa_p = oa_p.reshape(GDN_HEADS, bp * lp, LANES)
    attn_p = []
    kv_p = []
    for gi, jb in enumerate((4, 4, 2)):
        attn_p.append(_attn_prompt(proj_p, tabs_p, qw_rows[gi], kw_rows[gi], gi, bp, lp, jb))
        window = ATT_GROUPS[gi][0]
        kvt = _kv_tail(proj_p, tabs_p, kw_rows[gi], gi, bp, lp)
        kv_p.append(jnp.transpose(kvt.reshape(2 * ATT_CB, bp, window, LANES), (1, 2, 0, 3))
                    .reshape(1, bp, window, 2, ATT_HEADS, ATT_HD))
    h_p, hn_p, gate_p = _outproj(xp, oa_p, attn_p, w_out_b, n2, w_router, tm=512)
    y_p = _moe(hn_p, gate_p, h_p, wgu_b, wd_b, tm=1024)

    xs = x_sample.reshape(nb, D_MODEL)
    proj_s, ba_s = _inproj(xs, n1, w_main, w_ba, tm=nb, tn=INPROJ_TN)
    conv_state = jnp.transpose(state_conv[0], (1, 0, 2))
    sg_s, oa_s, conv_new = _gdn_sample(proj_s, ba_s, conv_state, conv_wt, alog_row, dtb_row, gnw, state_gdn[0])
    cos_s, sin_s = _rope_tables(np.full((1,), PAST_LEN))
    cos_s = jnp.asarray(np.broadcast_to(cos_s.T, (ROT_HALF, nb)))
    sin_s = jnp.asarray(np.broadcast_to(sin_s.T, (ROT_HALF, nb)))
    qw_cols = jnp.broadcast_to(jnp.tile(q_norm_w[0], (1, ATT_HEADS))[:, :, None], (3, ATT_W, nb))
    kw_cols = jnp.broadcast_to(jnp.tile(k_norm_w[0], (1, ATT_HEADS))[:, :, None], (3, ATT_W, nb))
    q_t, k_t, v_t = _attn_sample_prep(proj_s, qw_cols, kw_cols, cos_s, sin_s)
    attn_s = []
    kv_s = []
    for gi in range(3):
        window = ATT_GROUPS[gi][0]
        cache_t = jnp.transpose(caches[gi][0], (0, 2, 3, 4, 1)).reshape(nb, 2 * ATT_W, window)
        bt = math.gcd(nb, max(1, SAMPLE_BLOCK_BYTES // (2 * ATT_W * window * 4)))
        new_cache, o_t, lse_t = _attn_sample(cache_t, q_t, k_t, v_t, gi, bt)
        kv_s.append(jnp.transpose(new_cache.reshape(nb, 2, ATT_HEADS, ATT_HD, window), (0, 4, 1, 2, 3))[None])
        o_rows = jnp.transpose(o_t.reshape(ATT_CB, LANES, nb), (0, 2, 1))
        lse_rows = jnp.transpose(jnp.repeat(jnp.transpose(lse_t), ATT_HD, axis=1).reshape(nb, ATT_CB, LANES), (1, 0, 2))
        attn_s.append((o_rows, lse_rows))
    h_s, hn_s, gate_s = _outproj(xs, oa_s, attn_s, w_out_b, n2, w_router, tm=nb)
    y_s = _moe(hn_s, gate_s, h_s, wgu_b, wd_b, tm=nb)

    return (y_p.reshape(bp, lp, D_MODEL), y_s.reshape(nb, 1, D_MODEL),
            sg_p[None], sc_p[None], kv_p[0], kv_p[1], kv_p[2],
            sg_s[None], jnp.transpose(conv_new, (1, 0, 2))[None], kv_s[0], kv_s[1], kv_s[2])
```

```python
import functools
import math

import jax
import jax.numpy as jnp
import numpy as np
from jax import lax
from jax.experimental import pallas as pl
from jax.experimental.pallas import tpu as pltpu

F32 = jnp.float32
BF16 = jnp.bfloat16

LANES = 128
D_MODEL = 1024
GDN_HEADS = 4
GDN_D = 128
GDN_CONV = 4
GDN_CH = 3 * GDN_HEADS * GDN_D
GDN_CB = GDN_CH // LANES
GDN_BLOCKS = 16
ATT_GROUPS = ((128, 1), (512, 4), (2048, 16))
ATT_HEADS = 8
ATT_HD = 64
ATT_W = ATT_HEADS * ATT_HD
ATT_CB = ATT_W // LANES
SPAN = 128
ROT_DIM = 16
ROT_HALF = ROT_DIM // 2
ROPE_THETA = 500000.0
MOE_GROUPS = 4
MOE_PER_GROUP = 8
MOE_EXPERTS = 32
MOE_FF = 256
EPS = 1e-6
PAST_LEN = 8192
NEG = -1e30
N_MAIN_BLOCKS = GDN_BLOCKS + 3 * 3 * ATT_CB
INPROJ_TN = (N_MAIN_BLOCKS // 4) * LANES
VMEM_LIMIT = 48 * 1024 * 1024
MOE_VMEM_LIMIT = 56 * 1024 * 1024
SAMPLE_BLOCK_BYTES = 8 * 1024 * 1024


def _dot(a, b):
    return jnp.dot(a, b, preferred_element_type=F32)


def _dot_nt(a, b):
    return lax.dot_general(a, b, (((1,), (1,)), ((), ())), preferred_element_type=F32)


def _bdot(a, b):
    return _dot(a.astype(BF16), b.astype(BF16))


def _split3(x):
    hi = x.astype(BF16)
    r1 = x - hi.astype(F32)
    mid = r1.astype(BF16)
    lo = (r1 - mid.astype(F32)).astype(BF16)
    return hi, mid, lo


def _dot_sel_left(sel_bf16, x):
    hi, mid, lo = _split3(x)
    return _dot(sel_bf16, hi) + _dot(sel_bf16, mid) + _dot(sel_bf16, lo)


def _dot_sel_right(x, sel_bf16):
    hi, mid, lo = _split3(x)
    return _dot(hi, sel_bf16) + _dot(mid, sel_bf16) + _dot(lo, sel_bf16)


def _silu(x):
    return x * jax.nn.sigmoid(x)


def _softplus(x):
    return jnp.maximum(x, 0.0) + jnp.log1p(jnp.exp(-jnp.abs(x)))


def _params(sem, vmem=VMEM_LIMIT):
    return pltpu.CompilerParams(dimension_semantics=sem, vmem_limit_bytes=vmem)


def _inproj_kernel(x_ref, nw_ref, w_ref, wba_ref, out_ref, ba_ref, xn_ref, *, nblk):
    @pl.when(pl.program_id(1) == 0)
    def _():
        x = x_ref[...]
        ms = jnp.mean(x * x, axis=-1, keepdims=True)
        xn = ((x * lax.rsqrt(ms + EPS)) * nw_ref[...]).astype(BF16)
        xn_ref[...] = xn
        ba_ref[...] = _dot(xn, wba_ref[...])

    res = _dot(xn_ref[...], w_ref[...])
    for jj in range(nblk):
        out_ref[jj] = res[:, jj * LANES:(jj + 1) * LANES]


def _inproj(x2d, norm_w, w_main, w_ba, tm, tn=512):
    t = x2d.shape[0]
    ncol = w_main.shape[1]
    nblk = tn // LANES
    return pl.pallas_call(
        functools.partial(_inproj_kernel, nblk=nblk),
        grid=(t // tm, ncol // tn),
        in_specs=[
            pl.BlockSpec((tm, D_MODEL), lambda i, j: (i, 0)),
            pl.BlockSpec((1, D_MODEL), lambda i, j: (0, 0)),
            pl.BlockSpec((D_MODEL, tn), lambda i, j: (0, j)),
            pl.BlockSpec((D_MODEL, LANES), lambda i, j: (0, 0)),
        ],
        out_specs=[
            pl.BlockSpec((nblk, tm, LANES), lambda i, j: (j, i, 0)),
            pl.BlockSpec((tm, LANES), lambda i, j: (i, 0)),
        ],
        out_shape=[
            jax.ShapeDtypeStruct((ncol // LANES, t, LANES), F32),
            jax.ShapeDtypeStruct((t, LANES), F32),
        ],
        scratch_shapes=[pltpu.VMEM((tm, D_MODEL), BF16)],
        compiler_params=_params(("parallel", "arbitrary")),
        name="inproj",
    )(x2d, norm_w, w_main, w_ba)


def _gates(ba, alog_row, dtb_row):
    beta = jax.nn.sigmoid(ba)
    g = -jnp.exp(alog_row) * _softplus(ba + dtb_row)
    return beta, g


def _l2norm(x):
    return x * lax.rsqrt(jnp.sum(x * x, axis=-1, keepdims=True) + EPS)


def _gdn_out_norm(o, gnw, z):
    on = (o * lax.rsqrt(jnp.mean(o * o, axis=-1, keepdims=True) + EPS)) * gnw
    return on * _silu(z)


def _mm3(a, b):
    ah = a.astype(BF16)
    al = (a - ah.astype(F32)).astype(BF16)
    bh = b.astype(BF16)
    bl = (b - bh.astype(F32)).astype(BF16)
    return _dot(ah, bh) + _dot(ah, bl) + _dot(al, bh)


def _gdn_prompt_kernel(x_ref, ba_ref, cw_ref, alog_ref, dtb_ref, gnw_ref,
                       oa_ref, so_ref, co_ref, s_ref, cbuf_ref, *, nchunks, nseq, refine):
    c = pl.program_id(1)
    ch = SPAN

    @pl.when(c == 0)
    def _():
        s_ref[...] = jnp.zeros_like(s_ref)
        cbuf_ref[:, :, 0:8, :] = jnp.zeros((nseq, GDN_CB, 8, LANES), F32)

    cw = cw_ref[...]
    row = lax.broadcasted_iota(jnp.int32, (ch, ch), 0)
    col = lax.broadcasted_iota(jnp.int32, (ch, ch), 1)
    causal = row >= col
    strict = row > col
    tril = jnp.where(causal, 1.0, 0.0).astype(BF16)
    eye = jnp.where(row == col, 1.0, 0.0)

    @pl.when(c == nchunks - 1)
    def _():
        for sq in range(nseq):
            for cb in range(GDN_CB):
                co_ref[sq, :, cb * LANES:(cb + 1) * LANES] = x_ref[cb, sq, ch - 3:ch, :]

    chains = [(sq, h) for sq in range(nseq) for h in range(GDN_HEADS)]
    qs, ks, kbs, rhss, decays, egs, kdecs, elast = [], [], [], [], [], [], [], []
    for sq in range(nseq):
        qkv = []
        for cb in range(GDN_CB):
            cbuf_ref[sq, cb, 8:8 + ch, :] = x_ref[cb, sq]
            acc = None
            for tap in range(GDN_CONV):
                term = cw[tap:tap + 1, cb * LANES:(cb + 1) * LANES] * cbuf_ref[sq, cb, 5 + tap:5 + tap + ch, :]
                acc = term if acc is None else acc + term
            qkv.append(_silu(acc))
            cbuf_ref[sq, cb, 0:8, :] = cbuf_ref[sq, cb, ch:ch + 8, :]

        beta_t, g_t = _gates(ba_ref[sq], alog_ref[...], dtb_ref[...])
        gc = _dot_sel_left(tril, g_t)
        gct = jnp.transpose(gc)
        for h in range(GDN_HEADS):
            gl = GDN_HEADS + h
            gcol = gc[:, gl:gl + 1]
            grow = gct[gl:gl + 1, :]
            bcol = beta_t[:, h:h + 1]
            glast = gc[ch - 1:ch, gl:gl + 1]
            k = _l2norm(qkv[GDN_HEADS + h])
            kb = k * bcol
            eg = jnp.exp(gcol)
            qs.append(_l2norm(qkv[h]) * (GDN_D ** -0.5))
            ks.append(k.astype(BF16))
            kbs.append(kb.astype(BF16))
            rhss.append(jnp.concatenate([qkv[2 * GDN_HEADS + h] * bcol, kb * eg], axis=1))
            decays.append(jnp.exp(jnp.where(causal, gcol - grow, NEG)))
            egs.append(eg)
            kdecs.append(jnp.transpose(k * jnp.exp(glast - gcol)).astype(BF16))
            elast.append(jnp.exp(glast))

    n = len(chains)
    a_low = [jnp.where(strict, _dot_nt(kbs[i], ks[i]) * decays[i], 0.0) for i in range(n)]
    intra = [jnp.where(causal, _dot_nt(qs[i].astype(BF16), ks[i]) * decays[i], 0.0).astype(BF16) for i in range(n)]
    m = [-a for a in a_low]
    p = [eye + mi for mi in m]
    for _ in range(6):
        m = [_bdot(mi, mi) for mi in m]
        p = [pi + _bdot(pi, mi) for pi, mi in zip(p, m)]
    pb = [pi.astype(BF16) for pi in p]
    x = [_dot(pb[i], rhss[i].astype(BF16)) for i in range(n)]
    for _ in range(refine):
        res = [rhss[i] - x[i] - _mm3(a_low[i], x[i]) for i in range(n)]
        x = [x[i] + _dot(pb[i], res[i].astype(BF16)) for i in range(n)]
    sb = [s_ref[sq, h].astype(BF16) for sq, h in chains]
    v_new = [(x[i][:, :GDN_D] - _dot(x[i][:, GDN_D:].astype(BF16), sb[i])).astype(BF16) for i in range(n)]
    o = [_dot((qs[i] * egs[i]).astype(BF16), sb[i]) + _dot(intra[i], v_new[i]) for i in range(n)]
    for i, (sq, h) in enumerate(chains):
        s_ref[sq, h] = s_ref[sq, h] * elast[i] + _dot(kdecs[i], v_new[i])
        oa_ref[h, sq] = _gdn_out_norm(o[i], gnw_ref[...], x_ref[3 * GDN_HEADS + h, sq])

    @pl.when(c == nchunks - 1)
    def _():
        so_ref[...] = s_ref[...]


def _gdn_prompt(proj, ba, conv_wt, alog_row, dtb_row, gnw, batch, seq, nseq, refine=1):
    nchunks = seq // SPAN
    return pl.pallas_call(
        functools.partial(_gdn_prompt_kernel, nchunks=nchunks, nseq=nseq, refine=refine),
        grid=(batch // nseq, nchunks),
        in_specs=[
            pl.BlockSpec((GDN_BLOCKS, nseq, SPAN, LANES), lambda b, c: (0, b, c, 0)),
            pl.BlockSpec((nseq, SPAN, LANES), lambda b, c: (b, c, 0)),
            pl.BlockSpec((GDN_CONV, GDN_CH), lambda b, c: (0, 0)),
            pl.BlockSpec((1, LANES), lambda b, c: (0, 0)),
            pl.BlockSpec((1, LANES), lambda b, c: (0, 0)),
            pl.BlockSpec((1, LANES), lambda b, c: (0, 0)),
        ],
        out_specs=[
            pl.BlockSpec((GDN_HEADS, nseq, SPAN, LANES), lambda b, c: (0, b, c, 0)),
            pl.BlockSpec((nseq, GDN_HEADS, GDN_D, GDN_D), lambda b, c: (b, 0, 0, 0)),
            pl.BlockSpec((nseq, GDN_CONV - 1, GDN_CH), lambda b, c: (b, 0, 0)),
        ],
        out_shape=[
            jax.ShapeDtypeStruct((GDN_HEADS, batch, seq, LANES), F32),
            jax.ShapeDtypeStruct((batch, GDN_HEADS, GDN_D, GDN_D), F32),
            jax.ShapeDtypeStruct((batch, GDN_CONV - 1, GDN_CH), F32),
        ],
        scratch_shapes=[
            pltpu.VMEM((nseq, GDN_HEADS, GDN_D, GDN_D), F32),
            pltpu.VMEM((nseq, GDN_CB, SPAN + 8, LANES), F32),
        ],
        compiler_params=_params(("parallel", "arbitrary")),
        name="gdn_prompt",
    )(proj, ba, conv_wt, alog_row, dtb_row, gnw)


def _head_norm_rope(x, w, cos, sa, sb, seg_lo):
    sq = x * x
    s0 = jnp.sum(jnp.where(seg_lo, sq, 0.0), axis=1, keepdims=True)
    s1 = jnp.sum(jnp.where(seg_lo, 0.0, sq), axis=1, keepdims=True)
    ms = jnp.where(seg_lo, s0, s1) * (1.0 / ATT_HD)
    xn = (x * lax.rsqrt(ms + EPS)) * w
    return xn * cos + pltpu.roll(xn, LANES - ROT_HALF, 1) * sa + pltpu.roll(xn, ROT_HALF, 1) * sb


def _attn_prompt_kernel(q_ref, k_ref, v_ref, cos_ref, sa_ref, sb_ref, qw_ref, kw_ref,
                        o_ref, lse_ref, kprev_ref, vprev_ref, *, dil, jb, ru):
    n = pl.program_id(2)
    row = lax.broadcasted_iota(jnp.int32, (SPAN, 2 * SPAN), 0)
    col = lax.broadcasted_iota(jnp.int32, (SPAN, 2 * SPAN), 1)
    mask_inner = jnp.logical_or(col <= row, col - SPAN >= row)
    mask = jnp.logical_or(col <= row, jnp.logical_and(col - SPAN >= row, n > 0))
    lane = lax.broadcasted_iota(jnp.int32, (SPAN, LANES), 1)
    seg_lo = lane < ATT_HD
    lane2 = lax.broadcasted_iota(jnp.int32, (2 * SPAN, LANES), 1)
    in_head2 = [lane2 < ATT_HD, lane2 >= ATT_HD]
    er = lax.broadcasted_iota(jnp.int32, (LANES, LANES), 0)
    ec = lax.broadcasted_iota(jnp.int32, (LANES, LANES), 1)
    seg_sum = jnp.where((er < ATT_HD) == (ec < ATT_HD), 1.0, 0.0).astype(BF16)
    qw = qw_ref[...]
    kw = kw_ref[...]

    @pl.when(n == 0)
    def _():
        kprev_ref[...] = jnp.zeros_like(kprev_ref)
        vprev_ref[...] = jnp.zeros_like(vprev_ref)

    def body(it, carry):
        if dil > 1:
            rs = [it * ru + u for u in range(ru)]
            rows = [pl.ds(r, SPAN, stride=dil) for r in rs]
        else:
            rs = [0] * ru
            rows = [pl.ds(u * SPAN, SPAN) for u in range(ru)]
        blocks = [(u, j) for u in range(ru) for j in range(jb)]
        nblk = len(blocks)
        tabs = [(cos_ref[rows[u], :], sa_ref[rows[u], :], sb_ref[rows[u], :]) for u in range(ru)]
        xs = [q_ref[j, rows[u], :] for u, j in blocks] + [k_ref[j, rows[u], :] for u, j in blocks]
        ws = [qw] * nblk + [kw] * nblk
        sq = [x * x for x in xs]
        sq_hi = [s.astype(BF16) for s in sq]
        sq_lo = [(s - h.astype(F32)).astype(BF16) for s, h in zip(sq, sq_hi)]
        ssq = [_dot(h, seg_sum) + _dot(l, seg_sum) for h, l in zip(sq_hi, sq_lo)]
        xn = [(x * lax.rsqrt(s * (1.0 / ATT_HD) + EPS)) * w for x, s, w in zip(xs, ssq, ws)]
        roped = []
        for i, v in enumerate(xn):
            cos, sa, sb = tabs[blocks[i % nblk][0]]
            roped.append(v * cos + pltpu.roll(v, LANES - ROT_HALF, 1) * sa + pltpu.roll(v, ROT_HALF, 1) * sb)
        qb = [(roped[bi] * (ATT_HD ** -0.5)).astype(BF16) for bi in range(nblk)]
        kbs = [roped[nblk + bi].astype(BF16) for bi in range(nblk)]
        vbs = [v_ref[j, rows[u], :].astype(BF16) for u, j in blocks]
        kcat, vcat, masks = [], [], []
        for bi, (u, j) in enumerate(blocks):
            sl = slice(j * LANES, (j + 1) * LANES)
            if dil == 1 and u > 0:
                kp, vp = kbs[bi - jb], vbs[bi - jb]
                masks.append(mask_inner)
            else:
                kp, vp = kprev_ref[rs[u], :, sl], vprev_ref[rs[u], :, sl]
                masks.append(mask)
            kcat.append(jnp.concatenate([kbs[bi], kp], axis=0))
            vcat.append(jnp.concatenate([vbs[bi], vp], axis=0))
            if dil > 1 or u == ru - 1:
                kprev_ref[rs[u], :, sl] = kbs[bi]
                vprev_ref[rs[u], :, sl] = vbs[bi]
        heads = [(bi, hh) for bi in range(nblk) for hh in range(2)]
        in_head = [seg_lo, jnp.logical_not(seg_lo)]
        qh = [jnp.where(in_head[hh], qb[bi], jnp.zeros_like(qb[bi])) for bi, hh in heads]
        s = [jnp.where(masks[bi], _dot_nt(qh[i], kcat[bi]), NEG) for i, (bi, hh) in enumerate(heads)]
        mx = [jnp.max(a, axis=1, keepdims=True) for a in s]
        p = [jnp.exp(a - m).astype(BF16) for a, m in zip(s, mx)]
        ones = jnp.ones((2 * SPAN, LANES), BF16)
        acc = [_dot(p[i], jnp.where(in_head2[hh], vcat[bi], ones)) for i, (bi, hh) in enumerate(heads)]
        for bi, (u, j) in enumerate(blocks):
            a0, a1 = acc[2 * bi], acc[2 * bi + 1]
            den = pltpu.roll(jnp.where(seg_lo, a1, a0), ATT_HD, 1)
            o_ref[j, rows[u], :] = jnp.where(seg_lo, a0, a1) / den
            lse_ref[j, rows[u], :] = jnp.where(seg_lo, mx[2 * bi], mx[2 * bi + 1]) + jnp.log(den)
        return carry

    lax.fori_loop(0, dil // ru if dil > 1 else 1, body, 0)


def _attn_prompt(proj, tabs, qw, kw, gi, batch, seq, jb):
    window, dil = ATT_GROUPS[gi]
    njg = ATT_CB // jb
    ru = 2 * ATT_CB // jb
    sup = SPAN * (dil if dil > 1 else ru)
    assert dil == 1 or dil % ru == 0
    nb = seq // sup
    t = batch * seq
    base = (GDN_BLOCKS + gi * 3 * ATT_CB) // jb

    def proj_spec(which):
        return pl.BlockSpec((jb, sup, LANES), lambda b, jg, n: (base + which * njg + jg, b * nb + n, 0))

    tab_spec = pl.BlockSpec((sup, LANES), lambda b, jg, n: (n, 0))
    w_spec = pl.BlockSpec((1, LANES), lambda b, jg, n: (0, 0))
    return pl.pallas_call(
        functools.partial(_attn_prompt_kernel, dil=dil, jb=jb, ru=ru),
        grid=(batch, njg, nb),
        in_specs=[proj_spec(0), proj_spec(1), proj_spec(2), tab_spec, tab_spec, tab_spec, w_spec, w_spec],
        out_specs=[
            pl.BlockSpec((jb, sup, LANES), lambda b, jg, n: (jg, b * nb + n, 0)),
            pl.BlockSpec((jb, sup, LANES), lambda b, jg, n: (jg, b * nb + n, 0)),
        ],
        out_shape=[
            jax.ShapeDtypeStruct((ATT_CB, t, LANES), F32),
            jax.ShapeDtypeStruct((ATT_CB, t, LANES), F32),
        ],
        scratch_shapes=[
            pltpu.VMEM((dil, SPAN, jb * LANES), BF16),
            pltpu.VMEM((dil, SPAN, jb * LANES), BF16),
        ],
        compiler_params=_params(("parallel", "parallel", "arbitrary")),
        name=f"attn_prompt_w{window}",
    )(proj, proj, proj, tabs[0], tabs[1], tabs[2], qw, kw)


def _kv_tail_kernel(k_ref, v_ref, cos_ref, sa_ref, sb_ref, kw_ref, out_ref):
    rows = k_ref.shape[1]
    lane = lax.broadcasted_iota(jnp.int32, (rows, LANES), 1)
    seg_lo = lane < ATT_HD
    for j in range(ATT_CB):
        out_ref[j] = _head_norm_rope(k_ref[j], kw_ref[...], cos_ref[...], sa_ref[...], sb_ref[...], seg_lo)
        out_ref[ATT_CB + j] = v_ref[j]


def _kv_tail(proj, tabs, kw, gi, batch, seq):
    window, _ = ATT_GROUPS[gi]
    wt = min(window, 512)
    nt = window // wt
    per_seq = seq // wt
    base = (GDN_BLOCKS + gi * 3 * ATT_CB) // ATT_CB

    def rowblk(b, i):
        return b * per_seq + per_seq - nt + i

    tab_spec = pl.BlockSpec((wt, LANES), lambda b, i: (per_seq - nt + i, 0))
    return pl.pallas_call(
        _kv_tail_kernel,
        grid=(batch, nt),
        in_specs=[
            pl.BlockSpec((ATT_CB, wt, LANES), lambda b, i: (base + 1, rowblk(b, i), 0)),
            pl.BlockSpec((ATT_CB, wt, LANES), lambda b, i: (base + 2, rowblk(b, i), 0)),
            tab_spec, tab_spec, tab_spec,
            pl.BlockSpec((1, LANES), lambda b, i: (0, 0)),
        ],
        out_specs=pl.BlockSpec((2 * ATT_CB, wt, LANES), lambda b, i: (0, b * nt + i, 0)),
        out_shape=jax.ShapeDtypeStruct((2 * ATT_CB, batch * window, LANES), F32),
        compiler_params=_params(("parallel", "parallel")),
        name=f"kv_tail_w{window}",
    )(proj, proj, tabs[0], tabs[1], tabs[2], kw)


def _route(logits):
    lane = lax.broadcasted_iota(jnp.int32, logits.shape, 1).astype(F32)
    big = float(4 * LANES)
    is_g = jnp.logical_and(lane >= MOE_EXPERTS, lane < MOE_EXPERTS + MOE_GROUPS)
    gl = jnp.where(is_g, logits, NEG)
    gmax = jnp.max(gl, axis=1, keepdims=True)
    gidx = jnp.min(jnp.where(gl == gmax, lane, big), axis=1, keepdims=True) - MOE_EXPERTS
    p_group = 1.0 / jnp.sum(jnp.where(is_g, jnp.exp(gl - gmax), 0.0), axis=1, keepdims=True)
    lo = gidx * MOE_PER_GROUP
    in_grp = jnp.logical_and(lane >= lo, lane < lo + MOE_PER_GROUP)
    el = jnp.where(in_grp, logits, NEG)
    e1 = jnp.max(el, axis=1, keepdims=True)
    i1 = jnp.min(jnp.where(el == e1, lane, big), axis=1, keepdims=True)
    el2 = jnp.where(lane == i1, NEG, el)
    e2 = jnp.max(el2, axis=1, keepdims=True)
    i2 = jnp.min(jnp.where(el2 == e2, lane, big), axis=1, keepdims=True)
    t = jnp.exp(e2 - e1)
    w1 = 1.0 / (1.0 + t)
    w2 = t / (1.0 + t)
    return jnp.where(lane == i1, w1 * p_group, jnp.where(lane == i2, w2 * p_group, 0.0))


def _outproj_kernel(x_ref, oa_ref, o0_ref, l0_ref, o1_ref, l1_ref, o2_ref, l2_ref, wout_ref, n2_ref, wr_ref,
                    h_ref, hn_ref, gate_ref):
    obs = []
    for j in range(ATT_CB):
        l0, l1, l2 = l0_ref[j], l1_ref[j], l2_ref[j]
        mx = jnp.maximum(jnp.maximum(l0, l1), l2)
        e0 = jnp.exp(l0 - mx)
        e1 = jnp.exp(l1 - mx)
        e2 = jnp.exp(l2 - mx)
        ob = (e0 * o0_ref[j] + e1 * o1_ref[j] + e2 * o2_ref[j]) / (e0 + e1 + e2)
        obs.append(ob.astype(BF16))
    mix = jnp.concatenate([oa_ref[h].astype(BF16) for h in range(GDN_HEADS)] + obs, axis=1)
    acc = x_ref[...] + _dot(mix, wout_ref[...])
    h_ref[...] = acc
    hn = ((acc * lax.rsqrt(jnp.mean(acc * acc, axis=-1, keepdims=True) + EPS)) * n2_ref[...]).astype(BF16)
    hn_ref[...] = hn
    gate_ref[...] = _route(_dot(hn, wr_ref[...]))


def _outproj(x2d, oa, attn, w_out, norm2_w, w_router, tm):
    t = x2d.shape[0]
    specs = [pl.BlockSpec((tm, D_MODEL), lambda i: (i, 0)),
             pl.BlockSpec((GDN_HEADS, tm, LANES), lambda i: (0, i, 0))]
    args = [x2d, oa]
    for o, lse in attn:
        specs.append(pl.BlockSpec((ATT_CB, tm, LANES), lambda i: (0, i, 0)))
        specs.append(pl.BlockSpec((ATT_CB, tm, LANES), lambda i: (0, i, 0)))
        args += [o, lse]
    specs += [pl.BlockSpec((D_MODEL, D_MODEL), lambda i: (0, 0)),
              pl.BlockSpec((1, D_MODEL), lambda i: (0, 0)),
              pl.BlockSpec((D_MODEL, LANES), lambda i: (0, 0))]
    args += [w_out, norm2_w, w_router]
    return pl.pallas_call(
        _outproj_kernel,
        grid=(t // tm,),
        in_specs=specs,
        out_specs=[pl.BlockSpec((tm, D_MODEL), lambda i: (i, 0)),
                   pl.BlockSpec((tm, D_MODEL), lambda i: (i, 0)),
                   pl.BlockSpec((tm, LANES), lambda i: (i, 0))],
        out_shape=[jax.ShapeDtypeStruct((t, D_MODEL), F32),
                   jax.ShapeDtypeStruct((t, D_MODEL), BF16),
                   jax.ShapeDtypeStruct((t, LANES), F32)],
        compiler_params=_params(("parallel",)),
        name="outproj_router",
    )(*args)


def _moe_kernel(hn_ref, gate_ref, h_ref, wgu_ref, wd_ref, y_ref, acc_ref, *, eb, rsub):
    ei = pl.program_id(1)
    tm = hn_ref.shape[0]

    @pl.when(ei == 0)
    def _():
        acc_ref[...] = jnp.zeros_like(acc_ref)

    units = [(k, r) for k in range(eb) for r in range(tm // rsub)]

    def first_half(k, r):
        rows = slice(r * rsub, (r + 1) * rsub)
        return _dot(hn_ref[rows, :], wgu_ref[k])

    def second_half(k, r, gu):
        rows = slice(r * rsub, (r + 1) * rsub)
        hid = _silu(gu[:, :MOE_FF]) * gu[:, MOE_FF:]
        d = _dot(hid.astype(BF16), wd_ref[k])
        gate = gate_ref[rows, :]
        lane = lax.broadcasted_iota(jnp.int32, gate.shape, 1)
        gcol = jnp.sum(jnp.where(lane == ei * eb + k, gate, 0.0), axis=1, keepdims=True)
        acc_ref[rows, :] += gcol * d

    pending = None
    for k, r in units:
        gu = first_half(k, r)
        if pending is not None:
            second_half(*pending)
        pending = (k, r, gu)
    second_half(*pending)

    @pl.when(ei == pl.num_programs(1) - 1)
    def _():
        y_ref[...] = h_ref[...] + acc_ref[...]


def _moe(hn, gate, h, wgu_b, wd_b, tm, eb=8):
    t = hn.shape[0]
    return pl.pallas_call(
        functools.partial(_moe_kernel, eb=eb, rsub=min(tm, 256)),
        grid=(t // tm, MOE_EXPERTS // eb),
        in_specs=[
            pl.BlockSpec((tm, D_MODEL), lambda i, e: (i, 0)),
            pl.BlockSpec((tm, LANES), lambda i, e: (i, 0)),
            pl.BlockSpec((tm, D_MODEL), lambda i, e: (i, 0)),
            pl.BlockSpec((eb, D_MODEL, 2 * MOE_FF), lambda i, e: (e, 0, 0)),
            pl.BlockSpec((eb, MOE_FF, D_MODEL), lambda i, e: (e, 0, 0)),
        ],
        out_specs=pl.BlockSpec((tm, D_MODEL), lambda i, e: (i, 0)),
        out_shape=jax.ShapeDtypeStruct((t, D_MODEL), F32),
        scratch_shapes=[pltpu.VMEM((tm, D_MODEL), F32)],
        compiler_params=_params(("parallel", "arbitrary"), vmem=MOE_VMEM_LIMIT),
        name="moe",
    )(hn, gate, h, wgu_b, wd_b)


def _gdn_sample_kernel(x_ref, ba_ref, cs_ref, cw_ref, alog_ref, dtb_ref, gnw_ref, s_ref,
                       so_ref, oa_ref, co_ref, qt_ref, kt_ref, v_ref, beta_ref, g_ref, *, bt):
    b = pl.program_id(0)
    nb = x_ref.shape[1]

    @pl.when(b == 0)
    def _():
        cw = cw_ref[...]
        qkv = []
        for cb in range(GDN_CB):
            sl = slice(cb * LANES, (cb + 1) * LANES)
            u = x_ref[cb]
            acc = cw[3:4, sl] * u
            for tap in range(GDN_CONV - 1):
                acc = acc + cw[tap:tap + 1, sl] * cs_ref[tap, :, sl]
            qkv.append(_silu(acc))
            co_ref[0, :, sl] = cs_ref[1, :, sl]
            co_ref[1, :, sl] = cs_ref[2, :, sl]
            co_ref[2, :, sl] = u
        for h in range(GDN_HEADS):
            qt_ref[h] = jnp.transpose(_l2norm(qkv[h]) * (GDN_D ** -0.5))
            kt_ref[h] = jnp.transpose(_l2norm(qkv[GDN_HEADS + h]))
            v_ref[h] = qkv[2 * GDN_HEADS + h]
        beta_t, g_t = _gates(ba_ref[...], alog_ref[...], dtb_ref[...])
        beta_ref[...] = beta_t
        g_ref[...] = g_t

    chains = [(t, h) for t in range(bt) for h in range(GDN_HEADS)]
    lane_nb = lax.broadcasted_iota(jnp.int32, (GDN_D, nb), 1)
    is_b = [lane_nb == b * bt + t for t in range(bt)]
    brow = [beta_ref[pl.ds(b * bt + t, 1), :] for t in range(bt)]
    grow = [g_ref[pl.ds(b * bt + t, 1), :] for t in range(bt)]
    kcol = [jnp.sum(jnp.where(is_b[t], kt_ref[h], 0.0), axis=1, keepdims=True) for t, h in chains]
    qcol = [jnp.sum(jnp.where(is_b[t], qt_ref[h], 0.0), axis=1, keepdims=True) for t, h in chains]
    bet = [brow[t][:, h:h + 1] for t, h in chains]
    eg = [jnp.exp(grow[t][:, GDN_HEADS + h:GDN_HEADS + h + 1]) for t, h in chains]
    n = len(chains)
    ws = [jnp.sum((kcol[i] * (bet[i] * eg[i])) * s_ref[t, h], axis=0, keepdims=True) for i, (t, h) in enumerate(chains)]
    qs = [jnp.sum((qcol[i] * eg[i]) * s_ref[t, h], axis=0, keepdims=True) for i, (t, h) in enumerate(chains)]
    intra = [jnp.sum(qcol[i] * kcol[i], axis=0, keepdims=True) for i in range(n)]
    v_new = [v_ref[h, pl.ds(b * bt + t, 1), :] * bet[i] - ws[i] for i, (t, h) in enumerate(chains)]
    outs = [_gdn_out_norm(qs[i] + intra[i] * v_new[i], gnw_ref[...], x_ref[3 * GDN_HEADS + h, pl.ds(b * bt + t, 1), :])
            for i, (t, h) in enumerate(chains)]
    for i, (t, h) in enumerate(chains):
        so_ref[t, h] = s_ref[t, h] * eg[i] + kcol[i] * v_new[i]
        oa_ref[h, pl.ds(b * bt + t, 1), :] = outs[i]


def _gdn_sample(proj, ba, conv_state, conv_wt, alog_row, dtb_row, gnw, state, bt=8):
    nb = state.shape[0]
    full = lambda shape: pl.BlockSpec(shape, lambda b: (0,) * len(shape))
    return pl.pallas_call(
        functools.partial(_gdn_sample_kernel, bt=bt),
        grid=(nb // bt,),
        in_specs=[
            full((GDN_BLOCKS, nb, LANES)),
            full((nb, LANES)),
            full((GDN_CONV - 1, nb, GDN_CH)),
            full((GDN_CONV, GDN_CH)),
            full((1, LANES)), full((1, LANES)), full((1, LANES)),
            pl.BlockSpec((bt, GDN_HEADS, GDN_D, GDN_D), lambda b: (b, 0, 0, 0)),
        ],
        out_specs=[
            pl.BlockSpec((bt, GDN_HEADS, GDN_D, GDN_D), lambda b: (b, 0, 0, 0)),
            full((GDN_HEADS, nb, LANES)),
            full((GDN_CONV - 1, nb, GDN_CH)),
        ],
        out_shape=[
            jax.ShapeDtypeStruct((nb, GDN_HEADS, GDN_D, GDN_D), F32),
            jax.ShapeDtypeStruct((GDN_HEADS, nb, LANES), F32),
            jax.ShapeDtypeStruct((GDN_CONV - 1, nb, GDN_CH), F32),
        ],
        scratch_shapes=[
            pltpu.VMEM((GDN_HEADS, GDN_D, nb), F32),
            pltpu.VMEM((GDN_HEADS, GDN_D, nb), F32),
            pltpu.VMEM((GDN_HEADS, nb, GDN_D), F32),
            pltpu.VMEM((nb, LANES), F32),
            pltpu.VMEM((nb, LANES), F32),
        ],
        compiler_params=_params(("arbitrary",)),
        name="gdn_sample",
    )(proj, ba, conv_state, conv_wt, alog_row, dtb_row, gnw, state)


def _attn_sample_prep_kernel(xq_ref, xk_ref, xv_ref, qw_ref, kw_ref, cos_ref, sin_ref, q_ref, k_ref, v_ref):
    cos = cos_ref[...]
    sin = sin_ref[...]

    def prep(x_ref, w_ref, scale, out):
        for j in range(ATT_CB):
            xt = jnp.transpose(x_ref[j])
            for hh in range(2):
                xh = xt[hh * ATT_HD:(hh + 1) * ATT_HD]
                ms = jnp.mean(xh * xh, axis=0, keepdims=True)
                base = j * LANES + hh * ATT_HD
                xn = (xh * lax.rsqrt(ms + EPS)) * w_ref[base:base + ATT_HD, :]
                x1 = xn[0:ROT_HALF]
                x2 = xn[ROT_HALF:ROT_DIM]
                out[base:base + ROT_HALF, :] = (x1 * cos - x2 * sin) * scale
                out[base + ROT_HALF:base + ROT_DIM, :] = (x2 * cos + x1 * sin) * scale
                out[base + ROT_DIM:base + ATT_HD, :] = xn[ROT_DIM:] * scale

    prep(xq_ref, qw_ref, ATT_HD ** -0.5, q_ref)
    prep(xk_ref, kw_ref, 1.0, k_ref)
    for j in range(ATT_CB):
        v_ref[j * LANES:(j + 1) * LANES, :] = jnp.transpose(xv_ref[j])


def _attn_sample_prep(proj, qw_cols, kw_cols, cos_s, sin_s):
    nb = proj.shape[1]
    grp = lambda g: (g, 0, 0)
    return pl.pallas_call(
        _attn_sample_prep_kernel,
        grid=(3,),
        in_specs=[
            pl.BlockSpec((ATT_CB, nb, LANES), lambda g: (GDN_BLOCKS // ATT_CB + 3 * g, 0, 0)),
            pl.BlockSpec((ATT_CB, nb, LANES), lambda g: (GDN_BLOCKS // ATT_CB + 3 * g + 1, 0, 0)),
            pl.BlockSpec((ATT_CB, nb, LANES), lambda g: (GDN_BLOCKS // ATT_CB + 3 * g + 2, 0, 0)),
            pl.BlockSpec((None, ATT_W, nb), grp),
            pl.BlockSpec((None, ATT_W, nb), grp),
            pl.BlockSpec((ROT_HALF, nb), lambda g: (0, 0)),
            pl.BlockSpec((ROT_HALF, nb), lambda g: (0, 0)),
        ],
        out_specs=[pl.BlockSpec((None, ATT_W, nb), grp)] * 3,
        out_shape=[jax.ShapeDtypeStruct((3, ATT_W, nb), F32)] * 3,
        compiler_params=_params(("parallel",)),
        name="attn_sample_prep",
    )(proj, proj, proj, qw_cols, kw_cols, cos_s, sin_s)


def _attn_sample_kernel(c_ref, q_ref, k_ref, v_ref, co_ref, o_ref, lse_ref, *, dil, bt, nu):
    i = pl.program_id(0)
    w = c_ref.shape[2]
    nb = q_ref.shape[1]
    lane_w = lax.broadcasted_iota(jnp.int32, (ATT_W, w), 1)
    lane_nb = lax.broadcasted_iota(jnp.int32, (ATT_W, nb), 1)

    def shift_in(blk, new_col):
        return jnp.where(lane_w == w - 1, new_col, pltpu.roll(blk, w - 1, 1))

    @pl.when(i == 0)
    def _():
        o_ref[...] = jnp.zeros_like(o_ref)
        lse_ref[...] = jnp.zeros_like(lse_ref)

    pos = lax.broadcasted_iota(jnp.int32, (ATT_HEADS, w), 1)
    in_window = (pos & (dil - 1)) == 0
    lane8 = lax.broadcasted_iota(jnp.int32, (ATT_HEADS, nb), 1)

    def head_sums(x, axis):
        return [jnp.sum(x[h * ATT_HD:(h + 1) * ATT_HD], axis=axis, keepdims=True) for h in range(ATT_HEADS)]

    def step(it, carry):
        ts = [it * nu + u for u in range(nu)]
        bs = [i * bt + t for t in ts]
        lane_b = [lane_nb == b for b in bs]
        qcol = [jnp.sum(jnp.where(m, q_ref[...], 0.0), axis=1, keepdims=True) for m in lane_b]
        kcol = [jnp.sum(jnp.where(m, k_ref[...], 0.0), axis=1, keepdims=True) for m in lane_b]
        vcol = [jnp.sum(jnp.where(m, v_ref[...], 0.0), axis=1, keepdims=True) for m in lane_b]
        s = [jnp.concatenate(head_sums(c_ref[ts[u], 0:ATT_W, :] * qcol[u], 0), axis=0) for u in range(nu)]
        s_self = [jnp.concatenate(head_sums(qcol[u] * kcol[u], 0), axis=0) for u in range(nu)]
        s = [jnp.where(in_window, a, NEG) for a in s]
        mx = [jnp.maximum(jnp.max(a, axis=1, keepdims=True), b) for a, b in zip(s, s_self)]
        p = [jnp.exp(a - m) for a, m in zip(s, mx)]
        p_self = [jnp.exp(a - m) for a, m in zip(s_self, mx)]
        den = [jnp.sum(a, axis=1, keepdims=True) + b for a, b in zip(p, p_self)]
        p = [a / d for a, d in zip(p, den)]
        p_self = [a / d for a, d in zip(p_self, den)]
        ocol = []
        for u in range(nu):
            vblk = c_ref[ts[u], ATT_W:2 * ATT_W, :]
            parts = [jnp.sum(vblk[h * ATT_HD:(h + 1) * ATT_HD] * p[u][h:h + 1, :], axis=1, keepdims=True)
                     + p_self[u][h:h + 1, :] * vcol[u][h * ATT_HD:(h + 1) * ATT_HD] for h in range(ATT_HEADS)]
            ocol.append(jnp.concatenate(parts, axis=0))
        for u in range(nu):
            lse_ref[...] = jnp.where(lane8 == bs[u], mx[u] + jnp.log(den[u]), lse_ref[...])
            o_ref[...] = jnp.where(lane_b[u], ocol[u], o_ref[...])
            co_ref[ts[u], 0:ATT_W, :] = shift_in(c_ref[ts[u], 0:ATT_W, :], kcol[u])
            co_ref[ts[u], ATT_W:2 * ATT_W, :] = shift_in(c_ref[ts[u], ATT_W:2 * ATT_W, :], vcol[u])
        return carry

    lax.fori_loop(0, bt // nu, step, 0)


def _attn_sample(cache_t, q_t, k_t, v_t, gi, bt):
    window, dil = ATT_GROUPS[gi]
    nb = cache_t.shape[0]
    res = lambda shape: pl.BlockSpec(shape, lambda i: (gi, 0, 0))
    return pl.pallas_call(
        functools.partial(_attn_sample_kernel, dil=dil, bt=bt, nu=math.gcd(bt, 4)),
        grid=(nb // bt,),
        in_specs=[
            pl.BlockSpec((bt, 2 * ATT_W, window), lambda i: (i, 0, 0)),
            res((None, ATT_W, nb)), res((None, ATT_W, nb)), res((None, ATT_W, nb)),
        ],
        out_specs=[
            pl.BlockSpec((bt, 2 * ATT_W, window), lambda i: (i, 0, 0)),
            pl.BlockSpec((ATT_W, nb), lambda i: (0, 0)),
            pl.BlockSpec((ATT_HEADS, nb), lambda i: (0, 0)),
        ],
        out_shape=[
            jax.ShapeDtypeStruct(cache_t.shape, F32),
            jax.ShapeDtypeStruct((ATT_W, nb), F32),
            jax.ShapeDtypeStruct((ATT_HEADS, nb), F32),
        ],
        compiler_params=_params(("arbitrary",)),
        name=f"attn_sample_w{window}",
    )(cache_t, q_t, k_t, v_t)


def _rope_tables(pos):
    f32 = np.float32
    inv = np.exp(f32(-math.log(ROPE_THETA)) * np.arange(ROT_HALF, dtype=f32) * f32(2.0 / ROT_DIM)).astype(f32)
    ang = np.asarray(pos, dtype=f32)[:, None] * inv[None, :]
    return np.cos(ang).astype(f32), np.sin(ang).astype(f32)


def _rope_lane_tables(pos):
    cos, sin = _rope_tables(pos)
    n = len(pos)
    ones = np.ones((n, ATT_HD - ROT_DIM), np.float32)
    zeros = np.zeros((n, ATT_HD - ROT_DIM), np.float32)
    z8 = np.zeros((n, ROT_HALF), np.float32)
    cos_t = np.concatenate([cos, cos, ones], axis=1)
    sa_t = np.concatenate([-sin, z8, zeros], axis=1)
    sb_t = np.concatenate([z8, sin, zeros], axis=1)
    return tuple(jnp.asarray(np.concatenate([a, a], axis=1)) for a in (cos_t, sa_t, sb_t))


def kernel(x_prompt, x_sample, state_gdn, state_conv, cache_kv_w128, cache_kv_w512, cache_kv_w2048,
           norm1_w, w_in, conv_w, a_log, dt_bias, gdn_norm_w, q_norm_w, k_norm_w, w_out, norm2_w,
           w_router_group, w_router_expert, w_gate_up, w_down):
    bp, lp, _ = x_prompt.shape
    nb = x_sample.shape[0]
    assert x_sample.shape[1] == 1 and state_gdn.shape[0] == 1
    caches = (cache_kv_w128, cache_kv_w512, cache_kv_w2048)

    w_t = jnp.transpose(w_in[0])
    n_gdn = GDN_BLOCKS * LANES
    w_main = jnp.transpose(jnp.concatenate([w_t[:n_gdn], w_t[n_gdn + 2 * GDN_HEADS:]], axis=0)).astype(BF16)
    w_ba = jnp.pad(jnp.transpose(w_t[n_gdn:n_gdn + 2 * GDN_HEADS]), ((0, 0), (0, LANES - 2 * GDN_HEADS))).astype(BF16)
    conv_wt = jnp.transpose(conv_w[0])
    pad4 = lambda v: jnp.pad(v, (GDN_HEADS, LANES - 2 * GDN_HEADS))[None, :]
    alog_row = pad4(a_log[0])
    dtb_row = pad4(dt_bias[0])
    gnw = gdn_norm_w[0][None, :]
    n1 = norm1_w[0][None, :]
    n2 = norm2_w[0][None, :]
    w_out_b = w_out[0].astype(BF16)
    wgu_b = w_gate_up[0].astype(BF16)
    wd_b = w_down[0].astype(BF16)
    w_router = jnp.pad(jnp.concatenate([w_router_expert[0], w_router_group[0]], axis=1),
                       ((0, 0), (0, LANES - MOE_EXPERTS - MOE_GROUPS))).astype(BF16)
    qw_rows = [jnp.tile(q_norm_w[0, g], 2)[None, :] for g in range(3)]
    kw_rows = [jnp.tile(k_norm_w[0, g], 2)[None, :] for g in range(3)]
    tabs_p = _rope_lane_tables(np.arange(lp))

    xp = x_prompt.reshape(bp * lp, D_MODEL)
    proj_p, ba_p = _inproj(xp, n1, w_main, w_ba, tm=1024, tn=INPROJ_TN)
    oa_p, sg_p, sc_p = _gdn_prompt(proj_p.reshape(N_MAIN_BLOCKS, bp, lp, LANES), ba_p.reshape(bp, lp, LANES),
                                   conv_wt, alog_row, dtb_row, gnw, bp, lp, nseq=math.gcd(bp, 4))
    oa_p = oa_p.reshape(GDN_HEADS, bp * lp, LANES)
    attn_p = []
    kv_p = []
    for gi, jb in enumerate((4, 4, 2)):
        attn_p.append(_attn_prompt(proj_p, tabs_p, qw_rows[gi], kw_rows[gi], gi, bp, lp, jb))
        window = ATT_GROUPS[gi][0]
        kvt = _kv_tail(proj_p, tabs_p, kw_rows[gi], gi, bp, lp)
        kv_p.append(jnp.transpose(kvt.reshape(2 * ATT_CB, bp, window, LANES), (1, 2, 0, 3))
                    .reshape(1, bp, window, 2, ATT_HEADS, ATT_HD))
    h_p, hn_p, gate_p = _outproj(xp, oa_p, attn_p, w_out_b, n2, w_router, tm=512)
    y_p = _moe(hn_p, gate_p, h_p, wgu_b, wd_b, tm=1024)

    xs = x_sample.reshape(nb, D_MODEL)
    proj_s, ba_s = _inproj(xs, n1, w_main, w_ba, tm=nb, tn=INPROJ_TN)
    conv_state = jnp.transpose(state_conv[0], (1, 0, 2))
    sg_s, oa_s, conv_new = _gdn_sample(proj_s, ba_s, conv_state, conv_wt, alog_row, dtb_row, gnw, state_gdn[0])
    cos_s, sin_s = _rope_tables(np.full((1,), PAST_LEN))
    cos_s = jnp.asarray(np.broadcast_to(cos_s.T, (ROT_HALF, nb)))
    sin_s = jnp.asarray(np.broadcast_to(sin_s.T, (ROT_HALF, nb)))
    qw_cols = jnp.broadcast_to(jnp.tile(q_norm_w[0], (1, ATT_HEADS))[:, :, None], (3, ATT_W, nb))
    kw_cols = jnp.broadcast_to(jnp.tile(k_norm_w[0], (1, ATT_HEADS))[:, :, None], (3, ATT_W, nb))
    q_t, k_t, v_t = _attn_sample_prep(proj_s, qw_cols, kw_cols, cos_s, sin_s)
    attn_s = []
    kv_s = []
    for gi in range(3):
        window = ATT_GROUPS[gi][0]
        cache_t = jnp.transpose(caches[gi][0], (0, 2, 3, 4, 1)).reshape(nb, 2 * ATT_W, window)
        bt = math.gcd(nb, max(1, SAMPLE_BLOCK_BYTES // (2 * ATT_W * window * 4)))
        new_cache, o_t, lse_t = _attn_sample(cache_t, q_t, k_t, v_t, gi, bt)
        kv_s.append(jnp.transpose(new_cache.reshape(nb, 2, ATT_HEADS, ATT_HD, window), (0, 4, 1, 2, 3))[None])
        o_rows = jnp.transpose(o_t.reshape(ATT_CB, LANES, nb), (0, 2, 1))
        lse_rows = jnp.transpose(jnp.repeat(jnp.transpose(lse_t), ATT_HD, axis=1).reshape(nb, ATT_CB, LANES), (1, 0, 2))
        attn_s.append((o_rows, lse_rows))
    h_s, hn_s, gate_s = _outproj(xs, oa_s, attn_s, w_out_b, n2, w_router, tm=nb)
    y_s = _moe(hn_s, gate_s, h_s, wgu_b, wd_b, tm=nb)

    return (y_p.reshape(bp, lp, D_MODEL), y_s.reshape(nb, 1, D_MODEL),
            sg_p[None], sc_p[None], kv_p[0], kv_p[1], kv_p[2],
            sg_s[None], jnp.transpose(conv_new, (1, 0, 2))[None], kv_s[0], kv_s[1], kv_s[2])
```

```python
import functools
import math

import jax
import jax.numpy as jnp
import numpy as np
from jax import lax
from jax.experimental import pallas as pl
from jax.experimental.pallas import tpu as pltpu

F32 = jnp.float32
BF16 = jnp.bfloat16

LANES = 128
D_MODEL = 1024
GDN_HEADS = 4
GDN_D = 128
GDN_CONV = 4
GDN_CH = 3 * GDN_HEADS * GDN_D
GDN_CB = GDN_CH // LANES
GDN_BLOCKS = 16
GDN_BASE = 16
ATT_GROUPS = ((128, 1), (512, 4), (2048, 16))
ATT_HEADS = 8
ATT_HD = 64
ATT_W = ATT_HEADS * ATT_HD
ATT_CB = ATT_W // LANES
SPAN = 128
ROT_DIM = 16
ROT_HALF = ROT_DIM // 2
ROPE_THETA = 500000.0
MOE_GROUPS = 4
MOE_PER_GROUP = 8
MOE_EXPERTS = 32
MOE_FF = 256
EPS = 1e-6
PAST_LEN = 8192
NEG = -1e30
N_MAIN_BLOCKS = GDN_BLOCKS + 3 * 3 * ATT_CB
INPROJ_TN = (N_MAIN_BLOCKS // 4) * LANES
VMEM_LIMIT = 48 * 1024 * 1024
MOE_VMEM_LIMIT = 56 * 1024 * 1024
SAMPLE_BLOCK_BYTES = 8 * 1024 * 1024


def _dot(a, b):
    return jnp.dot(a, b, preferred_element_type=F32)


def _dot_nt(a, b):
    return lax.dot_general(a, b, (((1,), (1,)), ((), ())), preferred_element_type=F32)


def _bdot(a, b):
    return _dot(a.astype(BF16), b.astype(BF16))


def _split3(x):
    hi = x.astype(BF16)
    r1 = x - hi.astype(F32)
    mid = r1.astype(BF16)
    lo = (r1 - mid.astype(F32)).astype(BF16)
    return hi, mid, lo


def _dot_sel_left(sel_bf16, x):
    hi, mid, lo = _split3(x)
    return _dot(sel_bf16, hi) + _dot(sel_bf16, mid) + _dot(sel_bf16, lo)


def _dot_sel_right(x, sel_bf16):
    hi, mid, lo = _split3(x)
    return _dot(hi, sel_bf16) + _dot(mid, sel_bf16) + _dot(lo, sel_bf16)


def _silu(x):
    return x * jax.nn.sigmoid(x)


def _softplus(x):
    return jnp.maximum(x, 0.0) + jnp.log1p(jnp.exp(-jnp.abs(x)))


def _params(sem, vmem=VMEM_LIMIT):
    return pltpu.CompilerParams(dimension_semantics=sem, vmem_limit_bytes=vmem)


def _inproj_kernel(x_ref, nw_ref, w_ref, wba_ref, out_ref, ba_ref, xn_ref, *, nblk):
    @pl.when(pl.program_id(1) == 0)
    def _():
        x = x_ref[...]
        ms = jnp.mean(x * x, axis=-1, keepdims=True)
        xn = ((x * lax.rsqrt(ms + EPS)) * nw_ref[...]).astype(BF16)
        xn_ref[...] = xn
        ba_ref[...] = _dot(xn, wba_ref[...])

    res = _dot(xn_ref[...], w_ref[...])
    for jj in range(nblk):
        out_ref[jj] = res[:, jj * LANES:(jj + 1) * LANES]


def _inproj(x2d, norm_w, w_main, w_ba, tm, tn=512):
    t = x2d.shape[0]
    ncol = w_main.shape[1]
    nblk = tn // LANES
    return pl.pallas_call(
        functools.partial(_inproj_kernel, nblk=nblk),
        grid=(t // tm, ncol // tn),
        in_specs=[
            pl.BlockSpec((tm, D_MODEL), lambda i, j: (i, 0)),
            pl.BlockSpec((1, D_MODEL), lambda i, j: (0, 0)),
            pl.BlockSpec((D_MODEL, tn), lambda i, j: (0, j)),
            pl.BlockSpec((D_MODEL, LANES), lambda i, j: (0, 0)),
        ],
        out_specs=[
            pl.BlockSpec((nblk, tm, LANES), lambda i, j: (j, i, 0)),
            pl.BlockSpec((tm, LANES), lambda i, j: (i, 0)),
        ],
        out_shape=[
            jax.ShapeDtypeStruct((ncol // LANES, t, LANES), F32),
            jax.ShapeDtypeStruct((t, LANES), F32),
        ],
        scratch_shapes=[pltpu.VMEM((tm, D_MODEL), BF16)],
        compiler_params=_params(("parallel", "arbitrary")),
        name="inproj",
    )(x2d, norm_w, w_main, w_ba)


def _gates(ba, alog_row, dtb_row):
    beta = jax.nn.sigmoid(ba)
    g = -jnp.exp(alog_row) * _softplus(ba + dtb_row)
    return beta, g


def _l2norm(x):
    return x * lax.rsqrt(jnp.sum(x * x, axis=-1, keepdims=True) + EPS)


def _gdn_out_norm(o, gnw, z):
    on = (o * lax.rsqrt(jnp.mean(o * o, axis=-1, keepdims=True) + EPS)) * gnw
    return on * _silu(z)


def _mm3(a, b):
    ah = a.astype(BF16)
    al = (a - ah.astype(F32)).astype(BF16)
    bh = b.astype(BF16)
    bl = (b - bh.astype(F32)).astype(BF16)
    return _dot(ah, bh) + _dot(ah, bl) + _dot(al, bh)


def _gdn_prompt_kernel(x_ref, ba_ref, cw_ref, alog_ref, dtb_ref, gnw_ref,
                       oa_ref, so_ref, co_ref, s_ref, cbuf_ref, *, nchunks, nseq, refine):
    c = pl.program_id(1)
    ch = SPAN

    @pl.when(c == 0)
    def _():
        s_ref[...] = jnp.zeros_like(s_ref)
        cbuf_ref[:, :, 0:8, :] = jnp.zeros((nseq, GDN_CB, 8, LANES), F32)

    cw = cw_ref[...]
    row = lax.broadcasted_iota(jnp.int32, (ch, ch), 0)
    col = lax.broadcasted_iota(jnp.int32, (ch, ch), 1)
    causal = row >= col
    strict = row > col
    tril = jnp.where(causal, 1.0, 0.0).astype(BF16)
    eye = jnp.where(row == col, 1.0, 0.0)

    @pl.when(c == nchunks - 1)
    def _():
        for sq in range(nseq):
            for cb in range(GDN_CB):
                co_ref[sq, :, cb * LANES:(cb + 1) * LANES] = x_ref[cb, sq, ch - 3:ch, :]

    chains = [(sq, h) for sq in range(nseq) for h in range(GDN_HEADS)]
    qs, ks, kbs, rhss, decays, egs, kdecs, elast = [], [], [], [], [], [], [], []
    for sq in range(nseq):
        qkv = []
        for cb in range(GDN_CB):
            cbuf_ref[sq, cb, 8:8 + ch, :] = x_ref[cb, sq]
            acc = None
            for tap in range(GDN_CONV):
                term = cw[tap:tap + 1, cb * LANES:(cb + 1) * LANES] * cbuf_ref[sq, cb, 5 + tap:5 + tap + ch, :]
                acc = term if acc is None else acc + term
            qkv.append(_silu(acc))
            cbuf_ref[sq, cb, 0:8, :] = cbuf_ref[sq, cb, ch:ch + 8, :]

        beta_t, g_t = _gates(ba_ref[sq], alog_ref[...], dtb_ref[...])
        gc = _dot_sel_left(tril, g_t)
        gct = jnp.transpose(gc)
        for h in range(GDN_HEADS):
            gl = GDN_HEADS + h
            gcol = gc[:, gl:gl + 1]
            grow = gct[gl:gl + 1, :]
            bcol = beta_t[:, h:h + 1]
            glast = gc[ch - 1:ch, gl:gl + 1]
            k = _l2norm(qkv[GDN_HEADS + h])
            kb = k * bcol
            eg = jnp.exp(gcol)
            qs.append(_l2norm(qkv[h]) * (GDN_D ** -0.5))
            ks.append(k.astype(BF16))
            kbs.append(kb.astype(BF16))
            rhss.append(jnp.concatenate([qkv[2 * GDN_HEADS + h] * bcol, kb * eg], axis=1))
            decays.append(jnp.exp(jnp.where(causal, gcol - grow, NEG)))
            egs.append(eg)
            kdecs.append(jnp.transpose(k * jnp.exp(glast - gcol)).astype(BF16))
            elast.append(jnp.exp(glast))

    n = len(chains)
    a_low = [jnp.where(strict, _dot_nt(kbs[i], ks[i]) * decays[i], 0.0) for i in range(n)]
    intra = [jnp.where(causal, _dot_nt(qs[i].astype(BF16), ks[i]) * decays[i], 0.0).astype(BF16) for i in range(n)]
    def blk(x, s):
        return lax.shift_right_logical(x, int(math.log2(s)))

    def same_block(s):
        return blk(row, s) == blk(col, s)

    m = [jnp.where(same_block(GDN_BASE), -a, 0.0) for a in a_low]
    p = [eye + mi for mi in m]
    for _ in range(3):
        m = [_bdot(mi, mi) for mi in m]
        p = [pi + _bdot(pi, mi) for pi, mi in zip(p, m)]
    s = GDN_BASE
    while s < ch:
        lower_pair = jnp.logical_and(same_block(2 * s),
                                     jnp.logical_and((blk(row, s) & 1) == 1, (blk(col, s) & 1) == 0))
        lp = [_bdot(jnp.where(lower_pair, a, 0.0), pi) for a, pi in zip(a_low, p)]
        p = [pi - _bdot(pi, li) for pi, li in zip(p, lp)]
        s *= 2
    pb = [pi.astype(BF16) for pi in p]
    x = [_dot(pb[i], rhss[i].astype(BF16)) for i in range(n)]
    for _ in range(refine):
        res = [rhss[i] - x[i] - _mm3(a_low[i], x[i]) for i in range(n)]
        x = [x[i] + _dot(pb[i], res[i].astype(BF16)) for i in range(n)]
    sb = [s_ref[sq, h].astype(BF16) for sq, h in chains]
    v_new = [(x[i][:, :GDN_D] - _dot(x[i][:, GDN_D:].astype(BF16), sb[i])).astype(BF16) for i in range(n)]
    o = [_dot((qs[i] * egs[i]).astype(BF16), sb[i]) + _dot(intra[i], v_new[i]) for i in range(n)]
    for i, (sq, h) in enumerate(chains):
        s_ref[sq, h] = s_ref[sq, h] * elast[i] + _dot(kdecs[i], v_new[i])
        oa_ref[h, sq] = _gdn_out_norm(o[i], gnw_ref[...], x_ref[3 * GDN_HEADS + h, sq])

    @pl.when(c == nchunks - 1)
    def _():
        so_ref[...] = s_ref[...]


def _gdn_prompt(proj, ba, conv_wt, alog_row, dtb_row, gnw, batch, seq, nseq, refine=1):
    nchunks = seq // SPAN
    return pl.pallas_call(
        functools.partial(_gdn_prompt_kernel, nchunks=nchunks, nseq=nseq, refine=refine),
        grid=(batch // nseq, nchunks),
        in_specs=[
            pl.BlockSpec((GDN_BLOCKS, nseq, SPAN, LANES), lambda b, c: (0, b, c, 0)),
            pl.BlockSpec((nseq, SPAN, LANES), lambda b, c: (b, c, 0)),
            pl.BlockSpec((GDN_CONV, GDN_CH), lambda b, c: (0, 0)),
            pl.BlockSpec((1, LANES), lambda b, c: (0, 0)),
            pl.BlockSpec((1, LANES), lambda b, c: (0, 0)),
            pl.BlockSpec((1, LANES), lambda b, c: (0, 0)),
        ],
        out_specs=[
            pl.BlockSpec((GDN_HEADS, nseq, SPAN, LANES), lambda b, c: (0, b, c, 0)),
            pl.BlockSpec((nseq, GDN_HEADS, GDN_D, GDN_D), lambda b, c: (b, 0, 0, 0)),
            pl.BlockSpec((nseq, GDN_CONV - 1, GDN_CH), lambda b, c: (b, 0, 0)),
        ],
        out_shape=[
            jax.ShapeDtypeStruct((GDN_HEADS, batch, seq, LANES), F32),
            jax.ShapeDtypeStruct((batch, GDN_HEADS, GDN_D, GDN_D), F32),
            jax.ShapeDtypeStruct((batch, GDN_CONV - 1, GDN_CH), F32),
        ],
        scratch_shapes=[
            pltpu.VMEM((nseq, GDN_HEADS, GDN_D, GDN_D), F32),
            pltpu.VMEM((nseq, GDN_CB, SPAN + 8, LANES), F32),
        ],
        compiler_params=_params(("parallel", "arbitrary")),
        name="gdn_prompt",
    )(proj, ba, conv_wt, alog_row, dtb_row, gnw)


def _head_norm_rope(x, w, cos, sa, sb, seg_lo):
    sq = x * x
    s0 = jnp.sum(jnp.where(seg_lo, sq, 0.0), axis=1, keepdims=True)
    s1 = jnp.sum(jnp.where(seg_lo, 0.0, sq), axis=1, keepdims=True)
    ms = jnp.where(seg_lo, s0, s1) * (1.0 / ATT_HD)
    xn = (x * lax.rsqrt(ms + EPS)) * w
    return xn * cos + pltpu.roll(xn, LANES - ROT_HALF, 1) * sa + pltpu.roll(xn, ROT_HALF, 1) * sb


def _attn_prompt_kernel(q_ref, k_ref, v_ref, cos_ref, sa_ref, sb_ref, qw_ref, kw_ref,
                        o_ref, lse_ref, kprev_ref, vprev_ref, *, dil, jb, ru):
    n = pl.program_id(2)
    row = lax.broadcasted_iota(jnp.int32, (SPAN, 2 * SPAN), 0)
    col = lax.broadcasted_iota(jnp.int32, (SPAN, 2 * SPAN), 1)
    mask_inner = jnp.logical_or(col <= row, col - SPAN >= row)
    mask = jnp.logical_or(col <= row, jnp.logical_and(col - SPAN >= row, n > 0))
    lane = lax.broadcasted_iota(jnp.int32, (SPAN, LANES), 1)
    seg_lo = lane < ATT_HD
    lane2 = lax.broadcasted_iota(jnp.int32, (2 * SPAN, LANES), 1)
    in_head2 = [lane2 < ATT_HD, lane2 >= ATT_HD]
    er = lax.broadcasted_iota(jnp.int32, (LANES, LANES), 0)
    ec = lax.broadcasted_iota(jnp.int32, (LANES, LANES), 1)
    seg_sum = jnp.where((er < ATT_HD) == (ec < ATT_HD), 1.0, 0.0).astype(BF16)
    qw = qw_ref[...]
    kw = kw_ref[...]

    @pl.when(n == 0)
    def _():
        kprev_ref[...] = jnp.zeros_like(kprev_ref)
        vprev_ref[...] = jnp.zeros_like(vprev_ref)

    def body(it, carry):
        if dil > 1:
            rs = [it * ru + u for u in range(ru)]
            rows = [pl.ds(r, SPAN, stride=dil) for r in rs]
        else:
            rs = [0] * ru
            rows = [pl.ds(u * SPAN, SPAN) for u in range(ru)]
        blocks = [(u, j) for u in range(ru) for j in range(jb)]
        nblk = len(blocks)
        tabs = [(cos_ref[rows[u], :], sa_ref[rows[u], :], sb_ref[rows[u], :]) for u in range(ru)]
        xs = [q_ref[j, rows[u], :] for u, j in blocks] + [k_ref[j, rows[u], :] for u, j in blocks]
        ws = [qw] * nblk + [kw] * nblk
        sq = [x * x for x in xs]
        sq_hi = [s.astype(BF16) for s in sq]
        sq_lo = [(s - h.astype(F32)).astype(BF16) for s, h in zip(sq, sq_hi)]
        ssq = [_dot(h, seg_sum) + _dot(l, seg_sum) for h, l in zip(sq_hi, sq_lo)]
        xn = [(x * lax.rsqrt(s * (1.0 / ATT_HD) + EPS)) * w for x, s, w in zip(xs, ssq, ws)]
        roped = []
        for i, v in enumerate(xn):
            cos, sa, sb = tabs[blocks[i % nblk][0]]
            roped.append(v * cos + pltpu.roll(v, LANES - ROT_HALF, 1) * sa + pltpu.roll(v, ROT_HALF, 1) * sb)
        qb = [(roped[bi] * (ATT_HD ** -0.5)).astype(BF16) for bi in range(nblk)]
        kbs = [roped[nblk + bi].astype(BF16) for bi in range(nblk)]
        vbs = [v_ref[j, rows[u], :].astype(BF16) for u, j in blocks]
        kcat, vcat, masks = [], [], []
        for bi, (u, j) in enumerate(blocks):
            sl = slice(j * LANES, (j + 1) * LANES)
            if dil == 1 and u > 0:
                kp, vp = kbs[bi - jb], vbs[bi - jb]
                masks.append(mask_inner)
            else:
                kp, vp = kprev_ref[rs[u], :, sl], vprev_ref[rs[u], :, sl]
                masks.append(mask)
            kcat.append(jnp.concatenate([kbs[bi], kp], axis=0))
            vcat.append(jnp.concatenate([vbs[bi], vp], axis=0))
            if dil > 1 or u == ru - 1:
                kprev_ref[rs[u], :, sl] = kbs[bi]
                vprev_ref[rs[u], :, sl] = vbs[bi]
        heads = [(bi, hh) for bi in range(nblk) for hh in range(2)]
        in_head = [seg_lo, jnp.logical_not(seg_lo)]
        qh = [jnp.where(in_head[hh], qb[bi], jnp.zeros_like(qb[bi])) for bi, hh in heads]
        s = [jnp.where(masks[bi], _dot_nt(qh[i], kcat[bi]), NEG) for i, (bi, hh) in enumerate(heads)]
        mx = [jnp.max(a, axis=1, keepdims=True) for a in s]
        p = [jnp.exp(a - m).astype(BF16) for a, m in zip(s, mx)]
        ones = jnp.ones((2 * SPAN, LANES), BF16)
        acc = [_dot(p[i], jnp.where(in_head2[hh], vcat[bi], ones)) for i, (bi, hh) in enumerate(heads)]
        for bi, (u, j) in enumerate(blocks):
            a0, a1 = acc[2 * bi], acc[2 * bi + 1]
            den = pltpu.roll(jnp.where(seg_lo, a1, a0), ATT_HD, 1)
            o_ref[j, rows[u], :] = jnp.where(seg_lo, a0, a1) / den
            lse_ref[j, rows[u], :] = jnp.where(seg_lo, mx[2 * bi], mx[2 * bi + 1]) + jnp.log(den)
        return carry

    lax.fori_loop(0, dil // ru if dil > 1 else 1, body, 0)


def _attn_prompt(proj, tabs, qw, kw, gi, batch, seq, jb):
    window, dil = ATT_GROUPS[gi]
    njg = ATT_CB // jb
    ru = 2 * ATT_CB // jb
    sup = SPAN * (dil if dil > 1 else ru)
    assert dil == 1 or dil % ru == 0
    nb = seq // sup
    t = batch * seq
    base = (GDN_BLOCKS + gi * 3 * ATT_CB) // jb

    def proj_spec(which):
        return pl.BlockSpec((jb, sup, LANES), lambda b, jg, n: (base + which * njg + jg, b * nb + n, 0))

    tab_spec = pl.BlockSpec((sup, LANES), lambda b, jg, n: (n, 0))
    w_spec = pl.BlockSpec((1, LANES), lambda b, jg, n: (0, 0))
    return pl.pallas_call(
        functools.partial(_attn_prompt_kernel, dil=dil, jb=jb, ru=ru),
        grid=(batch, njg, nb),
        in_specs=[proj_spec(0), proj_spec(1), proj_spec(2), tab_spec, tab_spec, tab_spec, w_spec, w_spec],
        out_specs=[
            pl.BlockSpec((jb, sup, LANES), lambda b, jg, n: (jg, b * nb + n, 0)),
            pl.BlockSpec((jb, sup, LANES), lambda b, jg, n: (jg, b * nb + n, 0)),
        ],
        out_shape=[
            jax.ShapeDtypeStruct((ATT_CB, t, LANES), F32),
            jax.ShapeDtypeStruct((ATT_CB, t, LANES), F32),
        ],
        scratch_shapes=[
            pltpu.VMEM((dil, SPAN, jb * LANES), BF16),
            pltpu.VMEM((dil, SPAN, jb * LANES), BF16),
        ],
        compiler_params=_params(("parallel", "parallel", "arbitrary")),
        name=f"attn_prompt_w{window}",
    )(proj, proj, proj, tabs[0], tabs[1], tabs[2], qw, kw)


def _kv_tail_kernel(k_ref, v_ref, cos_ref, sa_ref, sb_ref, kw_ref, out_ref):
    rows = k_ref.shape[1]
    lane = lax.broadcasted_iota(jnp.int32, (rows, LANES), 1)
    seg_lo = lane < ATT_HD
    for j in range(ATT_CB):
        out_ref[j] = _head_norm_rope(k_ref[j], kw_ref[...], cos_ref[...], sa_ref[...], sb_ref[...], seg_lo)
        out_ref[ATT_CB + j] = v_ref[j]


def _kv_tail(proj, tabs, kw, gi, batch, seq):
    window, _ = ATT_GROUPS[gi]
    wt = min(window, 512)
    nt = window // wt
    per_seq = seq // wt
    base = (GDN_BLOCKS + gi * 3 * ATT_CB) // ATT_CB

    def rowblk(b, i):
        return b * per_seq + per_seq - nt + i

    tab_spec = pl.BlockSpec((wt, LANES), lambda b, i: (per_seq - nt + i, 0))
    return pl.pallas_call(
        _kv_tail_kernel,
        grid=(batch, nt),
        in_specs=[
            pl.BlockSpec((ATT_CB, wt, LANES), lambda b, i: (base + 1, rowblk(b, i), 0)),
            pl.BlockSpec((ATT_CB, wt, LANES), lambda b, i: (base + 2, rowblk(b, i), 0)),
            tab_spec, tab_spec, tab_spec,
            pl.BlockSpec((1, LANES), lambda b, i: (0, 0)),
        ],
        out_specs=pl.BlockSpec((2 * ATT_CB, wt, LANES), lambda b, i: (0, b * nt + i, 0)),
        out_shape=jax.ShapeDtypeStruct((2 * ATT_CB, batch * window, LANES), F32),
        compiler_params=_params(("parallel", "parallel")),
        name=f"kv_tail_w{window}",
    )(proj, proj, tabs[0], tabs[1], tabs[2], kw)


def _route(logits):
    lane = lax.broadcasted_iota(jnp.int32, logits.shape, 1).astype(F32)
    big = float(4 * LANES)
    is_g = jnp.logical_and(lane >= MOE_EXPERTS, lane < MOE_EXPERTS + MOE_GROUPS)
    gl = jnp.where(is_g, logits, NEG)
    gmax = jnp.max(gl, axis=1, keepdims=True)
    gidx = jnp.min(jnp.where(gl == gmax, lane, big), axis=1, keepdims=True) - MOE_EXPERTS
    p_group = 1.0 / jnp.sum(jnp.where(is_g, jnp.exp(gl - gmax), 0.0), axis=1, keepdims=True)
    lo = gidx * MOE_PER_GROUP
    in_grp = jnp.logical_and(lane >= lo, lane < lo + MOE_PER_GROUP)
    el = jnp.where(in_grp, logits, NEG)
    e1 = jnp.max(el, axis=1, keepdims=True)
    i1 = jnp.min(jnp.where(el == e1, lane, big), axis=1, keepdims=True)
    el2 = jnp.where(lane == i1, NEG, el)
    e2 = jnp.max(el2, axis=1, keepdims=True)
    i2 = jnp.min(jnp.where(el2 == e2, lane, big), axis=1, keepdims=True)
    t = jnp.exp(e2 - e1)
    w1 = 1.0 / (1.0 + t)
    w2 = t / (1.0 + t)
    return jnp.where(lane == i1, w1 * p_group, jnp.where(lane == i2, w2 * p_group, 0.0))


def _outproj_kernel(x_ref, oa_ref, o0_ref, l0_ref, o1_ref, l1_ref, o2_ref, l2_ref, wout_ref, n2_ref, wr_ref,
                    h_ref, hn_ref, gate_ref):
    obs = []
    for j in range(ATT_CB):
        l0, l1, l2 = l0_ref[j], l1_ref[j], l2_ref[j]
        mx = jnp.maximum(jnp.maximum(l0, l1), l2)
        e0 = jnp.exp(l0 - mx)
        e1 = jnp.exp(l1 - mx)
        e2 = jnp.exp(l2 - mx)
        ob = (e0 * o0_ref[j] + e1 * o1_ref[j] + e2 * o2_ref[j]) / (e0 + e1 + e2)
        obs.append(ob.astype(BF16))
    mix = jnp.concatenate([oa_ref[h].astype(BF16) for h in range(GDN_HEADS)] + obs, axis=1)
    acc = x_ref[...] + _dot(mix, wout_ref[...])
    h_ref[...] = acc
    hn = ((acc * lax.rsqrt(jnp.mean(acc * acc, axis=-1, keepdims=True) + EPS)) * n2_ref[...]).astype(BF16)
    hn_ref[...] = hn
    gate_ref[...] = _route(_dot(hn, wr_ref[...]))


def _outproj(x2d, oa, attn, w_out, norm2_w, w_router, tm):
    t = x2d.shape[0]
    specs = [pl.BlockSpec((tm, D_MODEL), lambda i: (i, 0)),
             pl.BlockSpec((GDN_HEADS, tm, LANES), lambda i: (0, i, 0))]
    args = [x2d, oa]
    for o, lse in attn:
        specs.append(pl.BlockSpec((ATT_CB, tm, LANES), lambda i: (0, i, 0)))
        specs.append(pl.BlockSpec((ATT_CB, tm, LANES), lambda i: (0, i, 0)))
        args += [o, lse]
    specs += [pl.BlockSpec((D_MODEL, D_MODEL), lambda i: (0, 0)),
              pl.BlockSpec((1, D_MODEL), lambda i: (0, 0)),
              pl.BlockSpec((D_MODEL, LANES), lambda i: (0, 0))]
    args += [w_out, norm2_w, w_router]
    return pl.pallas_call(
        _outproj_kernel,
        grid=(t // tm,),
        in_specs=specs,
        out_specs=[pl.BlockSpec((tm, D_MODEL), lambda i: (i, 0)),
                   pl.BlockSpec((tm, D_MODEL), lambda i: (i, 0)),
                   pl.BlockSpec((tm, LANES), lambda i: (i, 0))],
        out_shape=[jax.ShapeDtypeStruct((t, D_MODEL), F32),
                   jax.ShapeDtypeStruct((t, D_MODEL), BF16),
                   jax.ShapeDtypeStruct((t, LANES), F32)],
        compiler_params=_params(("parallel",)),
        name="outproj_router",
    )(*args)


def _moe_kernel(hn_ref, gate_ref, h_ref, wgu_ref, wd_ref, y_ref, acc_ref, *, eb, rsub):
    ei = pl.program_id(1)
    tm = hn_ref.shape[0]

    @pl.when(ei == 0)
    def _():
        acc_ref[...] = jnp.zeros_like(acc_ref)

    units = [(k, r) for k in range(eb) for r in range(tm // rsub)]

    def first_half(k, r):
        rows = slice(r * rsub, (r + 1) * rsub)
        return _dot(hn_ref[rows, :], wgu_ref[k])

    def second_half(k, r, gu):
        rows = slice(r * rsub, (r + 1) * rsub)
        hid = _silu(gu[:, :MOE_FF]) * gu[:, MOE_FF:]
        d = _dot(hid.astype(BF16), wd_ref[k])
        gate = gate_ref[rows, :]
        lane = lax.broadcasted_iota(jnp.int32, gate.shape, 1)
        gcol = jnp.sum(jnp.where(lane == ei * eb + k, gate, 0.0), axis=1, keepdims=True)
        acc_ref[rows, :] += gcol * d

    pending = None
    for k, r in units:
        gu = first_half(k, r)
        if pending is not None:
            second_half(*pending)
        pending = (k, r, gu)
    second_half(*pending)

    @pl.when(ei == pl.num_programs(1) - 1)
    def _():
        y_ref[...] = h_ref[...] + acc_ref[...]


def _moe(hn, gate, h, wgu_b, wd_b, tm, eb=8):
    t = hn.shape[0]
    return pl.pallas_call(
        functools.partial(_moe_kernel, eb=eb, rsub=min(tm, 256)),
        grid=(t // tm, MOE_EXPERTS // eb),
        in_specs=[
            pl.BlockSpec((tm, D_MODEL), lambda i, e: (i, 0)),
            pl.BlockSpec((tm, LANES), lambda i, e: (i, 0)),
            pl.BlockSpec((tm, D_MODEL), lambda i, e: (i, 0)),
            pl.BlockSpec((eb, D_MODEL, 2 * MOE_FF), lambda i, e: (e, 0, 0)),
            pl.BlockSpec((eb, MOE_FF, D_MODEL), lambda i, e: (e, 0, 0)),
        ],
        out_specs=pl.BlockSpec((tm, D_MODEL), lambda i, e: (i, 0)),
        out_shape=jax.ShapeDtypeStruct((t, D_MODEL), F32),
        scratch_shapes=[pltpu.VMEM((tm, D_MODEL), F32)],
        compiler_params=_params(("parallel", "arbitrary"), vmem=MOE_VMEM_LIMIT),
        name="moe",
    )(hn, gate, h, wgu_b, wd_b)


def _gdn_sample_kernel(x_ref, ba_ref, cs_ref, cw_ref, alog_ref, dtb_ref, gnw_ref, s_ref,
                       so_ref, oa_ref, co_ref, qt_ref, kt_ref, v_ref, beta_ref, g_ref, *, bt):
    b = pl.program_id(0)
    nb = x_ref.shape[1]

    @pl.when(b == 0)
    def _():
        cw = cw_ref[...]
        qkv = []
        for cb in range(GDN_CB):
            sl = slice(cb * LANES, (cb + 1) * LANES)
            u = x_ref[cb]
            acc = cw[3:4, sl] * u
            for tap in range(GDN_CONV - 1):
                acc = acc + cw[tap:tap + 1, sl] * cs_ref[tap, :, sl]
            qkv.append(_silu(acc))
            co_ref[0, :, sl] = cs_ref[1, :, sl]
            co_ref[1, :, sl] = cs_ref[2, :, sl]
            co_ref[2, :, sl] = u
        for h in range(GDN_HEADS):
            qt_ref[h] = jnp.transpose(_l2norm(qkv[h]) * (GDN_D ** -0.5))
            kt_ref[h] = jnp.transpose(_l2norm(qkv[GDN_HEADS + h]))
            v_ref[h] = qkv[2 * GDN_HEADS + h]
        beta_t, g_t = _gates(ba_ref[...], alog_ref[...], dtb_ref[...])
        beta_ref[...] = beta_t
        g_ref[...] = g_t

    chains = [(t, h) for t in range(bt) for h in range(GDN_HEADS)]
    lane_nb = lax.broadcasted_iota(jnp.int32, (GDN_D, nb), 1)
    is_b = [lane_nb == b * bt + t for t in range(bt)]
    brow = [beta_ref[pl.ds(b * bt + t, 1), :] for t in range(bt)]
    grow = [g_ref[pl.ds(b * bt + t, 1), :] for t in range(bt)]
    kcol = [jnp.sum(jnp.where(is_b[t], kt_ref[h], 0.0), axis=1, keepdims=True) for t, h in chains]
    qcol = [jnp.sum(jnp.where(is_b[t], qt_ref[h], 0.0), axis=1, keepdims=True) for t, h in chains]
    bet = [brow[t][:, h:h + 1] for t, h in chains]
    eg = [jnp.exp(grow[t][:, GDN_HEADS + h:GDN_HEADS + h + 1]) for t, h in chains]
    n = len(chains)
    ws = [jnp.sum((kcol[i] * (bet[i] * eg[i])) * s_ref[t, h], axis=0, keepdims=True) for i, (t, h) in enumerate(chains)]
    qs = [jnp.sum((qcol[i] * eg[i]) * s_ref[t, h], axis=0, keepdims=True) for i, (t, h) in enumerate(chains)]
    intra = [jnp.sum(qcol[i] * kcol[i], axis=0, keepdims=True) for i in range(n)]
    v_new = [v_ref[h, pl.ds(b * bt + t, 1), :] * bet[i] - ws[i] for i, (t, h) in enumerate(chains)]
    outs = [_gdn_out_norm(qs[i] + intra[i] * v_new[i], gnw_ref[...], x_ref[3 * GDN_HEADS + h, pl.ds(b * bt + t, 1), :])
            for i, (t, h) in enumerate(chains)]
    for i, (t, h) in enumerate(chains):
        so_ref[t, h] = s_ref[t, h] * eg[i] + kcol[i] * v_new[i]
        oa_ref[h, pl.ds(b * bt + t, 1), :] = outs[i]


def _gdn_sample(proj, ba, conv_state, conv_wt, alog_row, dtb_row, gnw, state, bt=8):
    nb = state.shape[0]
    full = lambda shape: pl.BlockSpec(shape, lambda b: (0,) * len(shape))
    return pl.pallas_call(
        functools.partial(_gdn_sample_kernel, bt=bt),
        grid=(nb // bt,),
        in_specs=[
            full((GDN_BLOCKS, nb, LANES)),
            full((nb, LANES)),
            full((GDN_CONV - 1, nb, GDN_CH)),
            full((GDN_CONV, GDN_CH)),
            full((1, LANES)), full((1, LANES)), full((1, LANES)),
            pl.BlockSpec((bt, GDN_HEADS, GDN_D, GDN_D), lambda b: (b, 0, 0, 0)),
        ],
        out_specs=[
            pl.BlockSpec((bt, GDN_HEADS, GDN_D, GDN_D), lambda b: (b, 0, 0, 0)),
            full((GDN_HEADS, nb, LANES)),
            full((GDN_CONV - 1, nb, GDN_CH)),
        ],
        out_shape=[
            jax.ShapeDtypeStruct((nb, GDN_HEADS, GDN_D, GDN_D), F32),
            jax.ShapeDtypeStruct((GDN_HEADS, nb, LANES), F32),
            jax.ShapeDtypeStruct((GDN_CONV - 1, nb, GDN_CH), F32),
        ],
        scratch_shapes=[
            pltpu.VMEM((GDN_HEADS, GDN_D, nb), F32),
            pltpu.VMEM((GDN_HEADS, GDN_D, nb), F32),
            pltpu.VMEM((GDN_HEADS, nb, GDN_D), F32),
            pltpu.VMEM((nb, LANES), F32),
            pltpu.VMEM((nb, LANES), F32),
        ],
        compiler_params=_params(("arbitrary",)),
        name="gdn_sample",
    )(proj, ba, conv_state, conv_wt, alog_row, dtb_row, gnw, state)


def _attn_sample_prep_kernel(xq_ref, xk_ref, xv_ref, qw_ref, kw_ref, cos_ref, sin_ref, q_ref, k_ref, v_ref):
    cos = cos_ref[...]
    sin = sin_ref[...]

    def prep(x_ref, w_ref, scale, out):
        for j in range(ATT_CB):
            xt = jnp.transpose(x_ref[j])
            for hh in range(2):
                xh = xt[hh * ATT_HD:(hh + 1) * ATT_HD]
                ms = jnp.mean(xh * xh, axis=0, keepdims=True)
                base = j * LANES + hh * ATT_HD
                xn = (xh * lax.rsqrt(ms + EPS)) * w_ref[base:base + ATT_HD, :]
                x1 = xn[0:ROT_HALF]
                x2 = xn[ROT_HALF:ROT_DIM]
                out[base:base + ROT_HALF, :] = (x1 * cos - x2 * sin) * scale
                out[base + ROT_HALF:base + ROT_DIM, :] = (x2 * cos + x1 * sin) * scale
                out[base + ROT_DIM:base + ATT_HD, :] = xn[ROT_DIM:] * scale

    prep(xq_ref, qw_ref, ATT_HD ** -0.5, q_ref)
    prep(xk_ref, kw_ref, 1.0, k_ref)
    for j in range(ATT_CB):
        v_ref[j * LANES:(j + 1) * LANES, :] = jnp.transpose(xv_ref[j])


def _attn_sample_prep(proj, qw_cols, kw_cols, cos_s, sin_s):
    nb = proj.shape[1]
    grp = lambda g: (g, 0, 0)
    return pl.pallas_call(
        _attn_sample_prep_kernel,
        grid=(3,),
        in_specs=[
            pl.BlockSpec((ATT_CB, nb, LANES), lambda g: (GDN_BLOCKS // ATT_CB + 3 * g, 0, 0)),
            pl.BlockSpec((ATT_CB, nb, LANES), lambda g: (GDN_BLOCKS // ATT_CB + 3 * g + 1, 0, 0)),
            pl.BlockSpec((ATT_CB, nb, LANES), lambda g: (GDN_BLOCKS // ATT_CB + 3 * g + 2, 0, 0)),
            pl.BlockSpec((None, ATT_W, nb), grp),
            pl.BlockSpec((None, ATT_W, nb), grp),
            pl.BlockSpec((ROT_HALF, nb), lambda g: (0, 0)),
            pl.BlockSpec((ROT_HALF, nb), lambda g: (0, 0)),
        ],
        out_specs=[pl.BlockSpec((None, ATT_W, nb), grp)] * 3,
        out_shape=[jax.ShapeDtypeStruct((3, ATT_W, nb), F32)] * 3,
        compiler_params=_params(("parallel",)),
        name="attn_sample_prep",
    )(proj, proj, proj, qw_cols, kw_cols, cos_s, sin_s)


def _attn_sample_kernel(c_ref, q_ref, k_ref, v_ref, co_ref, o_ref, lse_ref, *, dil, bt, nu):
    i = pl.program_id(0)
    w = c_ref.shape[2]
    nb = q_ref.shape[1]
    lane_w = lax.broadcasted_iota(jnp.int32, (ATT_W, w), 1)
    lane_nb = lax.broadcasted_iota(jnp.int32, (ATT_W, nb), 1)

    def shift_in(blk, new_col):
        return jnp.where(lane_w == w - 1, new_col, pltpu.roll(blk, w - 1, 1))

    @pl.when(i == 0)
    def _():
        o_ref[...] = jnp.zeros_like(o_ref)
        lse_ref[...] = jnp.zeros_like(lse_ref)

    pos = lax.broadcasted_iota(jnp.int32, (ATT_HEADS, w), 1)
    in_window = (pos & (dil - 1)) == 0
    lane8 = lax.broadcasted_iota(jnp.int32, (ATT_HEADS, nb), 1)

    def head_sums(x, axis):
        return [jnp.sum(x[h * ATT_HD:(h + 1) * ATT_HD], axis=axis, keepdims=True) for h in range(ATT_HEADS)]

    def step(it, carry):
        ts = [it * nu + u for u in range(nu)]
        bs = [i * bt + t for t in ts]
        lane_b = [lane_nb == b for b in bs]
        qcol = [jnp.sum(jnp.where(m, q_ref[...], 0.0), axis=1, keepdims=True) for m in lane_b]
        kcol = [jnp.sum(jnp.where(m, k_ref[...], 0.0), axis=1, keepdims=True) for m in lane_b]
        vcol = [jnp.sum(jnp.where(m, v_ref[...], 0.0), axis=1, keepdims=True) for m in lane_b]
        s = [jnp.concatenate(head_sums(c_ref[ts[u], 0:ATT_W, :] * qcol[u], 0), axis=0) for u in range(nu)]
        s_self = [jnp.concatenate(head_sums(qcol[u] * kcol[u], 0), axis=0) for u in range(nu)]
        s = [jnp.where(in_window, a, NEG) for a in s]
        mx = [jnp.maximum(jnp.max(a, axis=1, keepdims=True), b) for a, b in zip(s, s_self)]
        p = [jnp.exp(a - m) for a, m in zip(s, mx)]
        p_self = [jnp.exp(a - m) for a, m in zip(s_self, mx)]
        den = [jnp.sum(a, axis=1, keepdims=True) + b for a, b in zip(p, p_self)]
        p = [a / d for a, d in zip(p, den)]
        p_self = [a / d for a, d in zip(p_self, den)]
        ocol = []
        for u in range(nu):
            vblk = c_ref[ts[u], ATT_W:2 * ATT_W, :]
            parts = [jnp.sum(vblk[h * ATT_HD:(h + 1) * ATT_HD] * p[u][h:h + 1, :], axis=1, keepdims=True)
                     + p_self[u][h:h + 1, :] * vcol[u][h * ATT_HD:(h + 1) * ATT_HD] for h in range(ATT_HEADS)]
            ocol.append(jnp.concatenate(parts, axis=0))
        for u in range(nu):
            lse_ref[...] = jnp.where(lane8 == bs[u], mx[u] + jnp.log(den[u]), lse_ref[...])
            o_ref[...] = jnp.where(lane_b[u], ocol[u], o_ref[...])
            co_ref[ts[u], 0:ATT_W, :] = shift_in(c_ref[ts[u], 0:ATT_W, :], kcol[u])
            co_ref[ts[u], ATT_W:2 * ATT_W, :] = shift_in(c_ref[ts[u], ATT_W:2 * ATT_W, :], vcol[u])
        return carry

    lax.fori_loop(0, bt // nu, step, 0)


def _attn_sample(cache_t, q_t, k_t, v_t, gi, bt):
    window, dil = ATT_GROUPS[gi]
    nb = cache_t.shape[0]
    res = lambda shape: pl.BlockSpec(shape, lambda i: (gi, 0, 0))
    return pl.pallas_call(
        functools.partial(_attn_sample_kernel, dil=dil, bt=bt, nu=math.gcd(bt, 4)),
        grid=(nb // bt,),
        in_specs=[
            pl.BlockSpec((bt, 2 * ATT_W, window), lambda i: (i, 0, 0)),
            res((None, ATT_W, nb)), res((None, ATT_W, nb)), res((None, ATT_W, nb)),
        ],
        out_specs=[
            pl.BlockSpec((bt, 2 * ATT_W, window), lambda i: (i, 0, 0)),
            pl.BlockSpec((ATT_W, nb), lambda i: (0, 0)),
            pl.BlockSpec((ATT_HEADS, nb), lambda i: (0, 0)),
        ],
        out_shape=[
            jax.ShapeDtypeStruct(cache_t.shape, F32),
            jax.ShapeDtypeStruct((ATT_W, nb), F32),
            jax.ShapeDtypeStruct((ATT_HEADS, nb), F32),
        ],
        compiler_params=_params(("arbitrary",)),
        name=f"attn_sample_w{window}",
    )(cache_t, q_t, k_t, v_t)


def _rope_tables(pos):
    f32 = np.float32
    inv = np.exp(f32(-math.log(ROPE_THETA)) * np.arange(ROT_HALF, dtype=f32) * f32(2.0 / ROT_DIM)).astype(f32)
    ang = np.asarray(pos, dtype=f32)[:, None] * inv[None, :]
    return np.cos(ang).astype(f32), np.sin(ang).astype(f32)


def _rope_lane_tables(pos):
    cos, sin = _rope_tables(pos)
    n = len(pos)
    ones = np.ones((n, ATT_HD - ROT_DIM), np.float32)
    zeros = np.zeros((n, ATT_HD - ROT_DIM), np.float32)
    z8 = np.zeros((n, ROT_HALF), np.float32)
    cos_t = np.concatenate([cos, cos, ones], axis=1)
    sa_t = np.concatenate([-sin, z8, zeros], axis=1)
    sb_t = np.concatenate([z8, sin, zeros], axis=1)
    return tuple(jnp.asarray(np.concatenate([a, a], axis=1)) for a in (cos_t, sa_t, sb_t))


def kernel(x_prompt, x_sample, state_gdn, state_conv, cache_kv_w128, cache_kv_w512, cache_kv_w2048,
           norm1_w, w_in, conv_w, a_log, dt_bias, gdn_norm_w, q_norm_w, k_norm_w, w_out, norm2_w,
           w_router_group, w_router_expert, w_gate_up, w_down):
    bp, lp, _ = x_prompt.shape
    nb = x_sample.shape[0]
    assert x_sample.shape[1] == 1 and state_gdn.shape[0] == 1
    caches = (cache_kv_w128, cache_kv_w512, cache_kv_w2048)

    w_t = jnp.transpose(w_in[0])
    n_gdn = GDN_BLOCKS * LANES
    w_main = jnp.transpose(jnp.concatenate([w_t[:n_gdn], w_t[n_gdn + 2 * GDN_HEADS:]], axis=0)).astype(BF16)
    w_ba = jnp.pad(jnp.transpose(w_t[n_gdn:n_gdn + 2 * GDN_HEADS]), ((0, 0), (0, LANES - 2 * GDN_HEADS))).astype(BF16)
    conv_wt = jnp.transpose(conv_w[0])
    pad4 = lambda v: jnp.pad(v, (GDN_HEADS, LANES - 2 * GDN_HEADS))[None, :]
    alog_row = pad4(a_log[0])
    dtb_row = pad4(dt_bias[0])
    gnw = gdn_norm_w[0][None, :]
    n1 = norm1_w[0][None, :]
    n2 = norm2_w[0][None, :]
    w_out_b = w_out[0].astype(BF16)
    wgu_b = w_gate_up[0].astype(BF16)
    wd_b = w_down[0].astype(BF16)
    w_router = jnp.pad(jnp.concatenate([w_router_expert[0], w_router_group[0]], axis=1),
                       ((0, 0), (0, LANES - MOE_EXPERTS - MOE_GROUPS))).astype(BF16)
    qw_rows = [jnp.tile(q_norm_w[0, g], 2)[None, :] for g in range(3)]
    kw_rows = [jnp.tile(k_norm_w[0, g], 2)[None, :] for g in range(3)]
    tabs_p = _rope_lane_tables(np.arange(lp))

    xp = x_prompt.reshape(bp * lp, D_MODEL)
    proj_p, ba_p = _inproj(xp, n1, w_main, w_ba, tm=1024, tn=INPROJ_TN)
    oa_p, sg_p, sc_p = _gdn_prompt(proj_p.reshape(N_MAIN_BLOCKS, bp, lp, LANES), ba_p.reshape(bp, lp, LANES),
                                   conv_wt, alog_row, dtb_row, gnw, bp, lp, nseq=math.gcd(bp, 4))
    oa_p = oa_p.reshape(GDN_HEADS, bp * lp, LANES)
    attn_p = []
    kv_p = []
    for gi, jb in enumerate((4, 4, 2)):
        attn_p.append(_attn_prompt(proj_p, tabs_p, qw_rows[gi], kw_rows[gi], gi, bp, lp, jb))
        window = ATT_GROUPS[gi][0]
        kvt = _kv_tail(proj_p, tabs_p, kw_rows[gi], gi, bp, lp)
        kv_p.append(jnp.transpose(kvt.reshape(2 * ATT_CB, bp, window, LANES), (1, 2, 0, 3))
                    .reshape(1, bp, window, 2, ATT_HEADS, ATT_HD))
    h_p, hn_p, gate_p = _outproj(xp, oa_p, attn_p, w_out_b, n2, w_router, tm=512)
    y_p = _moe(hn_p, gate_p, h_p, wgu_b, wd_b, tm=1024)

    xs = x_sample.reshape(nb, D_MODEL)
    proj_s, ba_s = _inproj(xs, n1, w_main, w_ba, tm=nb, tn=INPROJ_TN)
    conv_state = jnp.transpose(state_conv[0], (1, 0, 2))
    sg_s, oa_s, conv_new = _gdn_sample(proj_s, ba_s, conv_state, conv_wt, alog_row, dtb_row, gnw, state_gdn[0])
    cos_s, sin_s = _rope_tables(np.full((1,), PAST_LEN))
    cos_s = jnp.asarray(np.broadcast_to(cos_s.T, (ROT_HALF, nb)))
    sin_s = jnp.asarray(np.broadcast_to(sin_s.T, (ROT_HALF, nb)))
    qw_cols = jnp.broadcast_to(jnp.tile(q_norm_w[0], (1, ATT_HEADS))[:, :, None], (3, ATT_W, nb))
    kw_cols = jnp.broadcast_to(jnp.tile(k_norm_w[0], (1, ATT_HEADS))[:, :, None], (3, ATT_W, nb))
    q_t, k_t, v_t = _attn_sample_prep(proj_s, qw_cols, kw_cols, cos_s, sin_s)
    attn_s = []
    kv_s = []
    for gi in range(3):
        window = ATT_GROUPS[gi][0]
        cache_t = jnp.transpose(caches[gi][0], (0, 2, 3, 4, 1)).reshape(nb, 2 * ATT_W, window)
        bt = math.gcd(nb, max(1, SAMPLE_BLOCK_BYTES // (2 * ATT_W * window * 4)))
        new_cache, o_t, lse_t = _attn_sample(cache_t, q_t, k_t, v_t, gi, bt)
        kv_s.append(jnp.transpose(new_cache.reshape(nb, 2, ATT_HEADS, ATT_HD, window), (0, 4, 1, 2, 3))[None])
        o_rows = jnp.transpose(o_t.reshape(ATT_CB, LANES, nb), (0, 2, 1))
        lse_rows = jnp.transpose(jnp.repeat(jnp.transpose(lse_t), ATT_HD, axis=1).reshape(nb, ATT_CB, LANES), (1, 0, 2))
        attn_s.append((o_rows, lse_rows))
    h_s, hn_s, gate_s = _outproj(xs, oa_s, attn_s, w_out_b, n2, w_router, tm=nb)
    y_s = _moe(hn_s, gate_s, h_s, wgu_b, wd_b, tm=nb)

    return (y_p.reshape(bp, lp, D_MODEL), y_s.reshape(nb, 1, D_MODEL),
            sg_p[None], sc_p[None], kv_p[0], kv_p[1], kv_p[2],
            sg_s[None], jnp.transpose(conv_new, (1, 0, 2))[None], kv_s[0], kv_s[1], kv_s[2])
```
